```python
import math
import jax, jax.numpy as jnp
from jax import lax
import numpy as np

D_MODEL = 2048
BATCH = 32
SEQ = 256
DEPTH = 2
DEC_BATCH = 8
DEC_SEQ = 4096
PAST_LEN = 512

GRID_W = 64
N_HEADS = 12
HEAD_DIM = 128
D_ATTN = N_HEADS * HEAD_DIM
D_SSM = 512
SSM_GROUP = 16
N_GROUPS = D_SSM // SSM_GROUP
STATE_DIM = 64
D_MIX = D_ATTN + D_SSM
D_IN = 3 * D_ATTN + D_SSM
WIN_H = 8
WIN_W = 16
Q_COL_BLOCK = 16
K_COL_BLOCK = 32
N_COL_BLOCKS = GRID_W // Q_COL_BLOCK
CTX_Q_BLOCK = 128
N_EXPERTS = 16
EXPERT_FF = 1024
CAPACITY_FACTOR = 2
EPS = 1e-6
NEG_INF = -1e30

kernel_name = "hymba_natten_s5_ec_diffusion_step"


def rmsnorm(x, g):
    xf = x.astype(jnp.float32)
    y = xf * lax.rsqrt(jnp.mean(xf * xf, axis=-1, keepdims=True) + EPS)
    return (y * g.astype(jnp.float32)).astype(x.dtype)


def adaln(cond, w_ada, b_ada):
    m = jax.nn.silu(cond) @ w_ada + b_ada
    return jnp.split(m[:, None, :], 6, axis=-1)


def modulate(x, shift, scale):
    return x * (1 + scale) + shift


def mixer_inputs(h, p):
    B, L, _ = h.shape
    z = h @ p['w_in']
    q = rmsnorm(z[..., :D_ATTN].reshape(B, L, N_HEADS, HEAD_DIM), p['q_norm'])
    k = rmsnorm(z[..., D_ATTN:2 * D_ATTN].reshape(B, L, N_HEADS, HEAD_DIM), p['k_norm'])
    v = z[..., 2 * D_ATTN:3 * D_ATTN].reshape(B, L, N_HEADS, HEAD_DIM)
    u = z[..., 3 * D_ATTN:]
    return q, k, v, u


def context_attention(q, k, v):
    B, L, H, dh = q.shape
    nb = L // CTX_Q_BLOCK
    qb = jnp.moveaxis(q.reshape(B, nb, CTX_Q_BLOCK, H, dh), 1, 0)
    scale = HEAD_DIM ** -0.5

    def block(qi):
        s = jnp.einsum('bqhd,bkhd->bhqk', qi, k).astype(jnp.float32) * scale
        pr = jax.nn.softmax(s, axis=-1).astype(v.dtype)
        return jnp.einsum('bhqk,bkhd->bqhd', pr, v)

    o = lax.map(block, qb)
    return jnp.moveaxis(o, 0, 1).reshape(B, L, H * dh)


def _col_tables():
    j = np.arange(N_COL_BLOCKS)
    k_start = np.clip(j * Q_COL_BLOCK - WIN_W // 2, 0, GRID_W - K_COL_BLOCK)
    key_cols = k_start[:, None] + np.arange(K_COL_BLOCK)[None, :]
    q_cols = j[:, None] * Q_COL_BLOCK + np.arange(Q_COL_BLOCK)[None, :]
    cs = np.clip(q_cols - WIN_W // 2, 0, GRID_W - WIN_W)
    kc = key_cols[:, None, :]
    mask = (kc >= cs[..., None]) & (kc < cs[..., None] + WIN_W)
    dc_idx = np.clip(kc - q_cols[..., None], -(WIN_W - 1), WIN_W - 1) + (WIN_W - 1)
    return key_cols, mask, dc_idx


def neighbourhood_attention(q, k, v, k_ctx, v_ctx, rpb):
    B, L, H, dh = q.shape
    rows = L // GRID_W
    kh = min(WIN_H, rows)
    key_cols, col_mask, dc_idx = _col_tables()
    qg = q.reshape(B, rows, N_COL_BLOCKS, Q_COL_BLOCK, H, dh)
    kg = k.reshape(B, rows, GRID_W, H, dh)
    vg = v.reshape(B, rows, GRID_W, H, dh)
    rpb_cols = rpb[:, :, dc_idx]
    mask = jnp.asarray(col_mask)[:, :, None, :]
    scale = HEAD_DIM ** -0.5
    n_loc = kh * K_COL_BLOCK

    def row_block(r):
        rs = jnp.clip(r - kh // 2, 0, rows - kh)
        kr = lax.dynamic_slice_in_dim(kg, rs, kh, axis=1)[:, :, key_cols]
        vr = lax.dynamic_slice_in_dim(vg, rs, kh, axis=1)[:, :, key_cols]
        qr = lax.dynamic_index_in_dim(qg, r, axis=1, keepdims=False)
        s_loc = jnp.einsum('bjqhd,bijkhd->bhjqik', qr, kr).astype(jnp.float32) * scale
        dr_idx = rs + jnp.arange(kh) - r + (WIN_H - 1)
        bias = jnp.transpose(rpb_cols[:, dr_idx], (0, 2, 3, 1, 4))
        s_loc = jnp.where(mask, s_loc + bias.astype(jnp.float32), NEG_INF)
        s_loc = s_loc.reshape(B, H, N_COL_BLOCKS, Q_COL_BLOCK, n_loc)
        s_ctx = jnp.einsum('bjqhd,bkhd->bhjqk', qr, k_ctx).astype(jnp.float32) * scale
        pr = jax.nn.softmax(jnp.concatenate([s_loc, s_ctx], axis=-1), axis=-1).astype(v.dtype)
        p_loc = pr[..., :n_loc].reshape(B, H, N_COL_BLOCKS, Q_COL_BLOCK, kh, K_COL_BLOCK)
        o = (jnp.einsum('bhjqik,bijkhd->bjqhd', p_loc, vr)
             + jnp.einsum('bhjqk,bkhd->bjqhd', pr[..., n_loc:], v_ctx))
        return o.reshape(B, GRID_W, H * dh)

    o = lax.map(row_block, jnp.arange(rows))
    return jnp.moveaxis(o, 0, 1).reshape(B, L, H * dh)


def ssm_discretize(p):
    a = lax.complex(p['ssm_a_re'].astype(jnp.float32), p['ssm_a_im'].astype(jnp.float32))
    dt = jnp.exp(p['ssm_log_dt'].astype(jnp.float32))[..., None]
    a_bar = jnp.exp(a * dt)
    b = lax.complex(p['ssm_b_re'].astype(jnp.float32), p['ssm_b_im'].astype(jnp.float32))
    b_bar = ((a_bar - 1) / a)[..., None] * b
    return a_bar, b_bar


def _ssm_combine(e1, e2):
    a1, b1 = e1
    a2, b2 = e2
    return a1 * a2, a2 * b1 + b2


def ssm_scan(uc, a_bar, b_bar, reverse):
    bu = jnp.einsum('blgc,gpc->blgp', uc, b_bar)
    a = jnp.broadcast_to(a_bar[None, None], (1, uc.shape[1]) + a_bar.shape)
    return lax.associative_scan(_ssm_combine, (a, bu), axis=1, reverse=reverse)


def s5_states(u, p, h0):
    B, L, _ = u.shape
    a_bar, b_bar = ssm_discretize(p)
    uc = u.astype(jnp.float32).reshape(B, L, N_GROUPS, SSM_GROUP).astype(jnp.complex64)
    hs = []
    for direction in range(2):
        a_cum, h = ssm_scan(uc, a_bar[direction], b_bar[direction], reverse=(direction == 1))
        if h0 is not None:
            h = h + a_cum * h0[:, direction][:, None]
        hs.append(h)
    return hs


def s5_readout(hs, u, p):
    B, L, _ = u.shape
    c = lax.complex(p['ssm_c_re'].astype(jnp.float32), p['ssm_c_im'].astype(jnp.float32))
    uf = u.astype(jnp.float32)
    y = p['ssm_d'].astype(jnp.float32) * uf
    for direction in range(2):
        y = y + jnp.einsum('blgp,gcp->blgc', hs[direction], c[direction]).real.reshape(B, L, D_SSM)
    g = jax.nn.gelu(y)
    out = g * jax.nn.sigmoid(g @ p['w_glu'].astype(jnp.float32) + p['b_glu'].astype(jnp.float32))
    return out.astype(u.dtype)


def expert_choice_ffn(x, p):
    B, L, D = x.shape
    cap = CAPACITY_FACTOR * L // N_EXPERTS
    aff = jax.nn.softmax((x @ p['w_router']).astype(jnp.float32), axis=-1)
    g, idx = lax.top_k(jnp.swapaxes(aff, 1, 2), cap)
    bidx = jnp.arange(B)[:, None, None]
    xs = x[bidx, idx]
    h = (jax.nn.silu(jnp.einsum('becd,edf->becf', xs, p['w_gate']))
         * jnp.einsum('becd,edf->becf', xs, p['w_up']))
    y = jnp.einsum('becf,efd->becd', h, p['w_down']) * g[..., None].astype(x.dtype)
    return jnp.zeros_like(x).at[bidx, idx].add(y)


def context_layer(x, mod, p):
    sh1, sc1, g1, sh2, sc2, g2 = mod
    h = modulate(rmsnorm(x, p['norm1']), sh1, sc1)
    q, k, v, u = mixer_inputs(h, p)
    attn = context_attention(q, k, v)
    hs = s5_states(u, p, None)
    finals = jnp.stack([hs[0][:, -1], hs[1][:, 0]], axis=1)
    ssm = s5_readout(hs, u, p)
    x = x + g1 * (jnp.concatenate([attn, ssm], axis=-1) @ p['w_out'])
    h = modulate(rmsnorm(x, p['norm2']), sh2, sc2)
    x = x + g2 * expert_choice_ffn(h, p)
    return x, k, v, finals


def latent_layer(x, mod, k_ctx, v_ctx, h0, p):
    sh1, sc1, g1, sh2, sc2, g2 = mod
    h = modulate(rmsnorm(x, p['norm1']), sh1, sc1)
    q, k, v, u = mixer_inputs(h, p)
    attn = neighbourhood_attention(q, k, v, k_ctx, v_ctx, p['rpb'])
    ssm = s5_readout(s5_states(u, p, h0), u, p)
    x = x + g1 * (jnp.concatenate([attn, ssm], axis=-1) @ p['w_out'])
    h = modulate(rmsnorm(x, p['norm2']), sh2, sc2)
    x = x + g2 * expert_choice_ffn(h, p)
    return x


def setup_inputs(seed: int = 0) -> dict:
    key = jax.random.key(seed)
    ks = jax.random.split(key, 32)
    f32 = jnp.float32

    def nrm(k, shape, s):
        return jax.random.normal(k, shape, f32) * s

    G, P = N_GROUPS, STATE_DIM
    n = jnp.arange(P, dtype=f32)
    return {
        "x_prompt": nrm(ks[0], (BATCH, SEQ, D_MODEL), 1.0),
        "x_sample": nrm(ks[1], (DEC_BATCH, DEC_SEQ, D_MODEL), 1.0),
        "cache_k": nrm(ks[2], (DEC_BATCH, DEPTH, PAST_LEN, N_HEADS, HEAD_DIM), 1.0),
        "cache_v": nrm(ks[3], (DEC_BATCH, DEPTH, PAST_LEN, N_HEADS, HEAD_DIM), 1.0),
        "state_ssm_re": nrm(ks[4], (DEC_BATCH, DEPTH, 2, G, P), 0.5),
        "state_ssm_im": nrm(ks[5], (DEC_BATCH, DEPTH, 2, G, P), 0.5),
        "c": nrm(ks[6], (DEC_BATCH, D_MODEL), 1.0),
        "c_ctx": nrm(ks[7], (D_MODEL,), 1.0),
        "w_ada": nrm(ks[8], (DEPTH, D_MODEL, 6 * D_MODEL), 0.5 * D_MODEL ** -0.5),
        "b_ada": nrm(ks[9], (DEPTH, 6 * D_MODEL), 0.02),
        "norm1_g": 1.0 + nrm(ks[10], (DEPTH, D_MODEL), 0.02),
        "norm2_g": 1.0 + nrm(ks[11], (DEPTH, D_MODEL), 0.02),
        "w_in": nrm(ks[12], (DEPTH, D_MODEL, D_IN), D_MODEL ** -0.5),
        "q_norm_g": 1.0 + nrm(ks[13], (DEPTH, HEAD_DIM), 0.02),
        "k_norm_g": 1.0 + nrm(ks[14], (DEPTH, HEAD_DIM), 0.02),
        "rel_pos_bias": nrm(ks[15], (DEPTH, N_HEADS, 2 * WIN_H - 1, 2 * WIN_W - 1), 0.5),
        "ssm_a_re": -0.5 + nrm(ks[16], (DEPTH, 2, G, P), 0.01),
        "ssm_a_im": math.pi * n + nrm(ks[17], (DEPTH, 2, G, P), 0.01),
        "ssm_log_dt": jax.random.uniform(ks[18], (DEPTH, 2, G), f32, math.log(0.001), math.log(0.1)),
        "ssm_b_re": nrm(ks[19], (DEPTH, 2, G, P, SSM_GROUP), SSM_GROUP ** -0.5),
        "ssm_b_im": nrm(ks[20], (DEPTH, 2, G, P, SSM_GROUP), SSM_GROUP ** -0.5),
        "ssm_c_re": nrm(ks[21], (DEPTH, 2, G, SSM_GROUP, P), P ** -0.5),
        "ssm_c_im": nrm(ks[22], (DEPTH, 2, G, SSM_GROUP, P), P ** -0.5),
        "ssm_d": nrm(ks[23], (DEPTH, D_SSM), 0.5),
        "w_glu": nrm(ks[24], (DEPTH, D_SSM, D_SSM), D_SSM ** -0.5),
        "b_glu": nrm(ks[25], (DEPTH, D_SSM), 0.02),
        "w_out": nrm(ks[26], (DEPTH, D_MIX, D_MODEL), D_MIX ** -0.5),
        "w_router": nrm(ks[27], (DEPTH, D_MODEL, N_EXPERTS), D_MODEL ** -0.5),
        "w_expert_gate": nrm(ks[28], (DEPTH, N_EXPERTS, D_MODEL, EXPERT_FF), D_MODEL ** -0.5),
        "w_expert_up": nrm(ks[29], (DEPTH, N_EXPERTS, D_MODEL, EXPERT_FF), D_MODEL ** -0.5),
        "w_expert_down": nrm(ks[30], (DEPTH, N_EXPERTS, EXPERT_FF, D_MODEL), EXPERT_FF ** -0.5),
    }


def reference(x_prompt, x_sample, cache_k, cache_v, state_ssm_re, state_ssm_im, c, c_ctx,
              w_ada, b_ada, norm1_g, norm2_g, w_in, q_norm_g, k_norm_g, rel_pos_bias,
              ssm_a_re, ssm_a_im, ssm_log_dt, ssm_b_re, ssm_b_im, ssm_c_re, ssm_c_im,
              ssm_d, w_glu, b_glu, w_out, w_router, w_expert_gate, w_expert_up, w_expert_down):
    xp = x_prompt
    xs = x_sample
    ks_out, vs_out, hre_out, him_out = [], [], [], []
    for l in range(DEPTH):
        p = dict(norm1=norm1_g[l], norm2=norm2_g[l], w_in=w_in[l], q_norm=q_norm_g[l],
                 k_norm=k_norm_g[l], rpb=rel_pos_bias[l], ssm_a_re=ssm_a_re[l],
                 ssm_a_im=ssm_a_im[l], ssm_log_dt=ssm_log_dt[l], ssm_b_re=ssm_b_re[l],
                 ssm_b_im=ssm_b_im[l], ssm_c_re=ssm_c_re[l], ssm_c_im=ssm_c_im[l],
                 ssm_d=ssm_d[l], w_glu=w_glu[l], b_glu=b_glu[l], w_out=w_out[l],
                 w_router=w_router[l], w_gate=w_expert_gate[l], w_up=w_expert_up[l],
                 w_down=w_expert_down[l])
        mod_ctx = adaln(c_ctx[None, :], w_ada[l], b_ada[l])
        mod_lat = adaln(c, w_ada[l], b_ada[l])
        xp, k_l, v_l, h_l = context_layer(xp, mod_ctx, p)
        ks_out.append(k_l)
        vs_out.append(v_l)
        hre_out.append(jnp.real(h_l))
        him_out.append(jnp.imag(h_l))
        h0 = lax.complex(state_ssm_re[:, l].astype(jnp.float32), state_ssm_im[:, l].astype(jnp.float32))
        xs = latent_layer(xs, mod_lat, cache_k[:, l], cache_v[:, l], h0, p)
    new_cache_k = jnp.stack(ks_out, axis=1)
    new_cache_v = jnp.stack(vs_out, axis=1)
    new_state_re = jnp.stack(hre_out, axis=1)
    new_state_im = jnp.stack(him_out, axis=1)
    return (xp, xs, new_cache_k, new_cache_v, new_state_re, new_state_im)
```

```python
import functools
import math

import numpy as np
import jax
import jax.numpy as jnp
from jax import lax
from jax.experimental import pallas as pl
from jax.experimental.pallas import tpu as pltpu

F32 = jnp.float32
BF16 = jnp.bfloat16
EPS = 1e-6
NEG_INF = -1e30

N_HEADS = 12
HEAD_DIM = 128
D_ATTN = N_HEADS * HEAD_DIM
SSM_GROUP = 16
N_GROUPS = 32
D_SSM = SSM_GROUP * N_GROUPS
STATE_DIM = 64
N_STATE = N_GROUPS * STATE_DIM
GRID_W = 64
WIN_H = 8
WIN_W = 16
N_EXPERTS = 16
CAPACITY_FACTOR = 2
SSM_CHUNK = 32
TOKEN_BLOCK = 128
SLOT_ALIGN = 16
VMEM_LIMIT = 56 * 1024 * 1024

_NT = (((1,), (1,)), ((), ()))


def _cp(sem, vmem=VMEM_LIMIT):
    return pltpu.CompilerParams(dimension_semantics=sem, vmem_limit_bytes=vmem)


def _dot(a, b, **kw):
    return jnp.dot(a, b, preferred_element_type=F32, **kw)


def _adaln_body(c_ref, w_ref, b_ref, o_ref):
    c = c_ref[...]
    s = (c * jax.nn.sigmoid(c)).astype(BF16)
    o_ref[...] = _dot(s, w_ref[...].astype(BF16)) + b_ref[...]


def adaln_mods(cond, w_ada, b_ada):
    depth, d, n = w_ada.shape
    r = cond.shape[0]
    tn = 1024
    return pl.pallas_call(
        _adaln_body,
        grid=(depth, n // tn),
        in_specs=[pl.BlockSpec((r, d), lambda l, j: (0, 0)),
                  pl.BlockSpec((None, d, tn), lambda l, j: (l, 0, j)),
                  pl.BlockSpec((None, 1, tn), lambda l, j: (l, 0, j))],
        out_specs=pl.BlockSpec((None, r, tn), lambda l, j: (l, 0, j)),
        out_shape=jax.ShapeDtypeStruct((depth, r, n), F32),
        compiler_params=_cp(("parallel", "parallel")),
    )(cond, w_ada, b_ada.reshape(depth, 1, n))


def _win_body(x_ref, sh_ref, sc_ref, g_ref, w_ref, qn_ref, kn_ref, qkv_ref, u_ref, h_scr, *, nq, tn):
    j = pl.program_id(1)

    @pl.when(j == 0)
    def _():
        x = x_ref[...]
        y = x * lax.rsqrt(jnp.mean(x * x, axis=-1, keepdims=True) + EPS) * g_ref[...]
        h_scr[...] = (y * (1.0 + sc_ref[...]) + sh_ref[...]).astype(BF16)

    acc = _dot(h_scr[...], w_ref[...])

    @pl.when(j < 2 * nq)
    def _():
        gain = jnp.where(j < nq, qn_ref[...], kn_ref[...])
        for c in range(tn // HEAD_DIM):
            sl = slice(c * HEAD_DIM, (c + 1) * HEAD_DIM)
            blk = acc[:, sl]
            r = blk * lax.rsqrt(jnp.mean(blk * blk, axis=-1, keepdims=True) + EPS) * gain
            qkv_ref[:, sl] = r.astype(qkv_ref.dtype)

    @pl.when((j >= 2 * nq) & (j < 3 * nq))
    def _():
        qkv_ref[...] = acc.astype(qkv_ref.dtype)

    @pl.when(j == 3 * nq)
    def _():
        u_ref[...] = acc


def in_proj(x, sh, sc, g, w, qn, kn, rows_per_mod, qkv_dtype, tm):
    m, d = x.shape
    tn = D_SSM
    nq = D_ATTN // tn
    nj = 3 * nq + 1
    mod_map = lambda i, j: (i * tm // rows_per_mod, 0, 0)
    return pl.pallas_call(
        functools.partial(_win_body, nq=nq, tn=tn),
        grid=(m // tm, nj),
        in_specs=[pl.BlockSpec((tm, d), lambda i, j: (i, 0)),
                  pl.BlockSpec((None, 1, d), mod_map),
                  pl.BlockSpec((None, 1, d), mod_map),
                  pl.BlockSpec((1, d), lambda i, j: (0, 0)),
                  pl.BlockSpec((d, tn), lambda i, j: (0, j)),
                  pl.BlockSpec((1, HEAD_DIM), lambda i, j: (0, 0)),
                  pl.BlockSpec((1, HEAD_DIM), lambda i, j: (0, 0))],
        out_specs=[pl.BlockSpec((tm, tn), lambda i, j: (i, jnp.minimum(j, 3 * nq - 1))),
                   pl.BlockSpec((tm, tn), lambda i, j: (i, 0))],
        out_shape=[jax.ShapeDtypeStruct((m, 3 * D_ATTN), qkv_dtype),
                   jax.ShapeDtypeStruct((m, D_SSM), F32)],
        scratch_shapes=[pltpu.VMEM((tm, d), BF16)],
        compiler_params=_cp(("parallel", "arbitrary")),
    )(x, sh, sc, g, w, qn, kn)


def _ctx_attn_body(q_ref, k_ref, v_ref, o_ref):
    scale = HEAD_DIM ** -0.5
    for h in range(N_HEADS):
        sl = slice(h * HEAD_DIM, (h + 1) * HEAD_DIM)
        q = q_ref[:, sl].astype(BF16)
        k = k_ref[:, sl].astype(BF16)
        v = v_ref[:, sl].astype(BF16)
        s = lax.dot_general(q, k, _NT, preferred_element_type=F32) * scale
        e = jnp.exp(s - jnp.max(s, axis=-1, keepdims=True))
        den = jnp.sum(e, axis=-1, keepdims=True)
        o_ref[:, sl] = (_dot(e.astype(BF16), v) / den).astype(o_ref.dtype)


def context_attention(qkv, batch, seq):
    m = qkv.shape[0]
    return pl.pallas_call(
        _ctx_attn_body,
        grid=(batch,),
        in_specs=[pl.BlockSpec((seq, D_ATTN), lambda b: (b, 0)),
                  pl.BlockSpec((seq, D_ATTN), lambda b: (b, 1)),
                  pl.BlockSpec((seq, D_ATTN), lambda b: (b, 2))],
        out_specs=pl.BlockSpec((seq, D_ATTN), lambda b: (b, 0)),
        out_shape=jax.ShapeDtypeStruct((m, D_ATTN), BF16),
        compiler_params=_cp(("parallel",)),
    )(qkv, qkv, qkv)


def natten_bias_table(rpb):
    qc = np.arange(GRID_W)[:, None]
    kc = np.arange(GRID_W)[None, :]
    cs = np.clip(qc - WIN_W // 2, 0, GRID_W - WIN_W)
    col_ok = (kc >= cs) & (kc < cs + WIN_W)
    dc = np.clip(kc - qc, -(WIN_W - 1), WIN_W - 1) + (WIN_W - 1)
    o = np.arange(WIN_H)[:, None]
    ki = np.arange(WIN_H)[None, :]
    dr = ki - o + (WIN_H - 1)
    t = rpb[:, dr][:, :, :, dc]
    t = jnp.where(jnp.asarray(col_ok)[None, None, None], t.astype(F32), NEG_INF)
    t = jnp.transpose(t, (0, 1, 3, 2, 4))
    return t.reshape(rpb.shape[0], WIN_H, GRID_W, WIN_H * GRID_W)


def _natten_body(q_ref, k_ref, v_ref, kc_ref, vc_ref, bias_ref, o_ref, *, rows):
    scale = HEAD_DIM ** -0.5
    nloc = WIN_H * GRID_W

    def row(r, carry):
        rs = jnp.clip(r - WIN_H // 2, 0, rows - WIN_H)
        qs = pl.ds(pl.multiple_of(r * GRID_W, GRID_W), GRID_W)
        ks = pl.ds(pl.multiple_of(rs * GRID_W, GRID_W), nloc)
        q = q_ref[qs, :]
        s_loc = lax.dot_general(q, k_ref[ks, :], _NT, preferred_element_type=F32) * scale + bias_ref[r - rs]
        s_ctx = lax.dot_general(q, kc_ref[...], _NT, preferred_element_type=F32) * scale
        m = jnp.maximum(jnp.max(s_loc, axis=-1, keepdims=True), jnp.max(s_ctx, axis=-1, keepdims=True))
        e_loc = jnp.exp(s_loc - m)
        e_ctx = jnp.exp(s_ctx - m)
        den = jnp.sum(e_loc, axis=-1, keepdims=True) + jnp.sum(e_ctx, axis=-1, keepdims=True)
        o = _dot(e_loc.astype(BF16), v_ref[ks, :]) + _dot(e_ctx.astype(BF16), vc_ref[...])
        o_ref[qs, :] = (o / den).astype(o_ref.dtype)
        return carry

    lax.fori_loop(0, rows, row, 0)


def neighbourhood_attention(qkv, k_ctx, v_ctx, bias, batch, seq):
    m = qkv.shape[0]
    rows = seq // GRID_W
    assert rows >= WIN_H and seq % GRID_W == 0
    past = k_ctx.shape[1]
    tok = pl.BlockSpec((seq, HEAD_DIM), lambda b, h: (b, h))
    return pl.pallas_call(
        functools.partial(_natten_body, rows=rows),
        grid=(batch, N_HEADS),
        in_specs=[tok,
                  pl.BlockSpec((seq, HEAD_DIM), lambda b, h: (b, N_HEADS + h)),
                  pl.BlockSpec((seq, HEAD_DIM), lambda b, h: (b, 2 * N_HEADS + h)),
                  pl.BlockSpec((None, past, HEAD_DIM), lambda b, h: (b, 0, h)),
                  pl.BlockSpec((None, past, HEAD_DIM), lambda b, h: (b, 0, h)),
                  pl.BlockSpec((None, WIN_H, GRID_W, WIN_H * GRID_W), lambda b, h: (h, 0, 0, 0))],
        out_specs=tok,
        out_shape=jax.ShapeDtypeStruct((m, D_ATTN), BF16),
        compiler_params=_cp(("parallel", "parallel")),
    )(qkv, qkv, qkv, k_ctx, v_ctx, bias)


def _bmm_body(a_ref, b_ref, o_ref, *, precision):
    a = a_ref[...]
    b = b_ref[...]
    if precision is None:
        a = a.astype(BF16)
        b = b.astype(BF16)
    o_ref[...] = _dot(a, b, precision=precision)


def bmm(a, b, tm, precision=None):
    g, m, k = a.shape
    n = b.shape[2]
    return pl.pallas_call(
        functools.partial(_bmm_body, precision=precision),
        grid=(g, m // tm),
        in_specs=[pl.BlockSpec((None, tm, k), lambda gi, i: (gi, i, 0)),
                  pl.BlockSpec((None, k, n), lambda gi, i: (gi, 0, 0))],
        out_specs=pl.BlockSpec((None, tm, n), lambda gi, i: (gi, i, 0)),
        out_shape=jax.ShapeDtypeStruct((g, m, n), F32),
        compiler_params=_cp(("parallel", "parallel")),
    )(a, b)


def ssm_operators(p, t):
    g, pp, c = N_GROUPS, STATE_DIM, SSM_GROUP
    a_re = p['ssm_a_re'].astype(F32)
    a_im = p['ssm_a_im'].astype(F32)
    dt = jnp.exp(p['ssm_log_dt'].astype(F32))[..., None]
    tau = jnp.arange(t + 1, dtype=F32)[:, None, None, None]
    mag = jnp.exp(tau * (dt * a_re))
    ph = tau * (dt * a_im)
    pr, pi = mag * jnp.cos(ph), mag * jnp.sin(ph)
    ar, ai = pr[1], pi[1]
    den = a_re * a_re + a_im * a_im
    qr = ((ar - 1.0) * a_re + ai * a_im) / den
    qi = (ai * a_re - (ar - 1.0) * a_im) / den
    b_re, b_im = p['ssm_b_re'].astype(F32), p['ssm_b_im'].astype(F32)
    bbr = qr[..., None] * b_re - qi[..., None] * b_im
    bbi = qr[..., None] * b_im + qi[..., None] * b_re
    ctr = jnp.swapaxes(p['ssm_c_re'].astype(F32), -1, -2)
    cti = jnp.swapaxes(p['ssm_c_im'].astype(F32), -1, -2)

    bcr = bbr[..., :, None] * ctr[..., None, :] - bbi[..., :, None] * cti[..., None, :]
    bci = bbr[..., :, None] * cti[..., None, :] + bbi[..., :, None] * ctr[..., None, :]
    rhs = jnp.concatenate([bcr, bci], axis=2).reshape(2 * g, 2 * pp, c * c)
    pw = jnp.concatenate([pr[:t], -pi[:t]], axis=-1)
    lhs = jnp.transpose(pw, (1, 2, 0, 3)).reshape(2 * g, t, 2 * pp)
    kf = bmm(lhs, rhs, tm=t, precision=lax.Precision.HIGHEST).reshape(2, g, t, c, c)
    kcat = jnp.concatenate([jnp.flip(kf[1], axis=1)[:, :t - 1], kf[0][:, :1] + kf[1][:, :1], kf[0][:, 1:]], axis=1)
    idx = np.arange(t)[None, :] - np.arange(t)[:, None] + (t - 1)
    m_intra = jnp.transpose(kcat[:, idx], (0, 1, 3, 2, 4)).reshape(g, c * t, c * t)

    def state_in(pwr_r, pwr_i, d):
        br = jnp.swapaxes(bbr[d], -1, -2)[:, None]
        bi = jnp.swapaxes(bbi[d], -1, -2)[:, None]
        er = jnp.transpose(pwr_r, (1, 0, 2))[:, :, None]
        ei = jnp.transpose(pwr_i, (1, 0, 2))[:, :, None]
        return er * br - ei * bi, er * bi + ei * br

    wfr, wfi = state_in(pr[:t, 0][::-1], pi[:t, 0][::-1], 0)
    wbr, wbi = state_in(pr[:t, 1], pi[:t, 1], 1)
    w_state = jnp.concatenate([wfr, wfi, wbr, wbi], axis=-1).reshape(g, c * t, 4 * pp)

    def state_out(pwr_r, pwr_i, d):
        cr = jnp.swapaxes(ctr[d], -1, -2)[:, None]
        ci = jnp.swapaxes(cti[d], -1, -2)[:, None]
        er = jnp.transpose(pwr_r, (1, 0, 2))[:, :, None]
        ei = jnp.transpose(pwr_i, (1, 0, 2))[:, :, None]
        vr = cr * er - ci * ei
        vi = -(cr * ei + ci * er)
        to_rows = lambda v: jnp.transpose(v, (0, 3, 1, 2)).reshape(g, pp, c * t)
        return to_rows(vr), to_rows(vi)

    vfr, vfi = state_out(pr[1:t + 1, 0], pi[1:t + 1, 0], 0)
    vbr, vbi = state_out(pr[1:t + 1, 1][::-1], pi[1:t + 1, 1][::-1], 1)
    v_state = jnp.concatenate([vfr, vfi, vbr, vbi], axis=1)
    d_tile = jnp.tile(p['ssm_d'].astype(F32).reshape(g, 1, c), (1, 1, t))
    a_chunk = jnp.stack([pr[t, 0], pi[t, 0], pr[t, 1], pi[t, 1]]).reshape(4, 1, g * pp)
    return w_state.astype(BF16), m_intra.astype(BF16), v_state.astype(BF16), d_tile, a_chunk


def _ssm_rec_body(loc_ref, a_ref, h0_ref, prev_ref, fin_ref, *, nc):
    def run(o, reverse):
        ar, ai = a_ref[o], a_ref[o + 1]

        def step(i, carry):
            n = nc - 1 - i if reverse else i
            sr, si = carry
            prev_ref[o, n] = sr
            prev_ref[o + 1, n] = si
            return ar * sr - ai * si + loc_ref[o, n], ar * si + ai * sr + loc_ref[o + 1, n]

        sr, si = lax.fori_loop(0, nc, step, (h0_ref[o], h0_ref[o + 1]))
        fin_ref[o] = sr
        fin_ref[o + 1] = si

    run(0, False)
    run(2, True)


def ssm_chunk_recurrence(loc, a_chunk, h0):
    _, nc, b, s = loc.shape
    tl = 256
    return pl.pallas_call(
        functools.partial(_ssm_rec_body, nc=nc),
        grid=(s // tl,),
        in_specs=[pl.BlockSpec((4, nc, b, tl), lambda i: (0, 0, 0, i)),
                  pl.BlockSpec((4, 1, tl), lambda i: (0, 0, i)),
                  pl.BlockSpec((4, b, tl), lambda i: (0, 0, i))],
        out_specs=[pl.BlockSpec((4, nc, b, tl), lambda i: (0, 0, 0, i)),
                   pl.BlockSpec((4, b, tl), lambda i: (0, 0, i))],
        out_shape=[jax.ShapeDtypeStruct(loc.shape, F32), jax.ShapeDtypeStruct(h0.shape, F32)],
        compiler_params=_cp(("parallel",)),
    )(loc, a_chunk, h0)


def _ssm_out_body(u_ref, s_ref, m_ref, v_ref, d_ref, o_ref):
    u = u_ref[...]
    y = _dot(u.astype(BF16), m_ref[...]) + _dot(s_ref[...].astype(BF16), v_ref[...])
    o_ref[...] = y + d_ref[...] * u


def ssm_chunk_output(uc, sp, m_intra, v_state, d_tile, tm):
    g, n, k = uc.shape
    ks = sp.shape[2]
    return pl.pallas_call(
        _ssm_out_body,
        grid=(g, n // tm),
        in_specs=[pl.BlockSpec((None, tm, k), lambda gi, i: (gi, i, 0)),
                  pl.BlockSpec((None, tm, ks), lambda gi, i: (gi, i, 0)),
                  pl.BlockSpec((None, k, k), lambda gi, i: (gi, 0, 0)),
                  pl.BlockSpec((None, ks, k), lambda gi, i: (gi, 0, 0)),
                  pl.BlockSpec((None, 1, k), lambda gi, i: (gi, 0, 0))],
        out_specs=pl.BlockSpec((None, tm, k), lambda gi, i: (gi, i, 0)),
        out_shape=jax.ShapeDtypeStruct((g, n, k), F32),
        compiler_params=_cp(("parallel", "parallel")),
    )(uc, sp, m_intra, v_state, d_tile)


def s5_mix(u, batch, seq, ops, h0):
    w_state, m_intra, v_state, d_tile, a_chunk = ops
    t, g, c, pp = SSM_CHUNK, N_GROUPS, SSM_GROUP, STATE_DIM
    nc = seq // t
    n = batch * nc
    tm = min(n, 512)
    uc = jnp.transpose(u.reshape(batch, nc, t, g, c), (3, 0, 1, 2, 4)).reshape(g, n, t * c)
    loc = bmm(uc, w_state, tm=tm)
    loc = jnp.transpose(loc.reshape(g, batch, nc, 4, pp), (3, 2, 1, 0, 4)).reshape(4, nc, batch, g * pp)
    prev, fin = ssm_chunk_recurrence(loc, a_chunk, h0)
    sp = jnp.transpose(prev.reshape(4, nc, batch, g, pp), (3, 2, 1, 0, 4)).reshape(g, n, 4 * pp)
    y = ssm_chunk_output(uc, sp, m_intra, v_state, d_tile, tm)
    y = jnp.transpose(y.reshape(g, batch, nc, t, c), (1, 2, 3, 0, 4)).reshape(batch * seq, g * c)
    return y, fin


def _gelu_tanh(x):
    return 0.5 * x * (1.0 + jnp.tanh(math.sqrt(2.0 / math.pi) * (x + 0.044715 * (x * x * x))))


def _wout_body(attn_ref, y_ref, x_ref, g1_ref, sh2_ref, sc2_ref, wglu_ref, bglu_ref, wo_ref, n2_ref, wr_ref,
               xm_ref, h2_ref, aff_ref):
    g = _gelu_tanh(y_ref[...])
    ssm = g * jax.nn.sigmoid(_dot(g.astype(BF16), wglu_ref[...]) + bglu_ref[...])
    o = _dot(attn_ref[...], wo_ref[:D_ATTN, :]) + _dot(ssm.astype(BF16), wo_ref[D_ATTN:, :])
    xm = x_ref[...] + g1_ref[...] * o
    xm_ref[...] = xm
    hn = xm * lax.rsqrt(jnp.mean(xm * xm, axis=-1, keepdims=True) + EPS) * n2_ref[...]
    h2 = (hn * (1.0 + sc2_ref[...]) + sh2_ref[...]).astype(BF16)
    h2_ref[...] = h2
    logits = lax.dot_general(wr_ref[...], h2, _NT, preferred_element_type=F32)
    e = jnp.exp(logits - jnp.max(logits, axis=0, keepdims=True))
    aff_ref[...] = e / jnp.sum(e, axis=0, keepdims=True)


def out_proj(attn, y, x, g1, sh2, sc2, w_glu, b_glu, w_out, n2, w_router_t, rows_per_mod, tm):
    m, d = x.shape
    ne = w_router_t.shape[0]
    mod_map = lambda i: (i * tm // rows_per_mod, 0, 0)
    const = lambda i: (0, 0)
    return pl.pallas_call(
        _wout_body,
        grid=(m // tm,),
        in_specs=[pl.BlockSpec((tm, D_ATTN), lambda i: (i, 0)),
                  pl.BlockSpec((tm, D_SSM), lambda i: (i, 0)),
                  pl.BlockSpec((tm, d), lambda i: (i, 0)),
                  pl.BlockSpec((None, 1, d), mod_map),
                  pl.BlockSpec((None, 1, d), mod_map),
                  pl.BlockSpec((None, 1, d), mod_map),
                  pl.BlockSpec((D_SSM, D_SSM), const),
                  pl.BlockSpec((1, D_SSM), const),
                  pl.BlockSpec((D_ATTN + D_SSM, d), const),
                  pl.BlockSpec((1, d), const),
                  pl.BlockSpec((ne, d), const)],
        out_specs=[pl.BlockSpec((tm, d), lambda i: (i, 0)),
                   pl.BlockSpec((tm, d), lambda i: (i, 0)),
                   pl.BlockSpec((ne, tm), lambda i: (0, i))],
        out_shape=[jax.ShapeDtypeStruct((m, d), F32),
                   jax.ShapeDtypeStruct((m, d), BF16),
                   jax.ShapeDtypeStruct((ne, m), F32)],
        compiler_params=_cp(("parallel",)),
    )(attn, y, x, g1, sh2, sc2, w_glu, b_glu, w_out, n2, w_router_t)


def _select_body(a_ref, pos_ref, cnt_ref, *, cap, nblk):
    v = a_ref[...]
    vb = pltpu.bitcast(v, jnp.int32)
    ne = v.shape[0]

    def bit_step(i, thr):
        cand = thr | (jnp.int32(1) << (30 - i))
        n_ge = jnp.sum(jnp.where(vb >= cand, 1.0, 0.0), axis=1, keepdims=True)
        return jnp.where(n_ge >= cap, cand, thr)

    thr = lax.fori_loop(0, 31, bit_step, jnp.zeros((ne, 1), jnp.int32))
    need = cap - jnp.sum(jnp.where(vb > thr, 1.0, 0.0), axis=1, keepdims=True)
    tb = TOKEN_BLOCK
    tri = jnp.where(lax.broadcasted_iota(jnp.int32, (tb, tb), 0) < lax.broadcasted_iota(jnp.int32, (tb, tb), 1),
                    1.0, 0.0).astype(BF16)
    lane = lax.broadcasted_iota(jnp.int32, (ne, 128), 1)
    eq_base = jnp.zeros((ne, 1), F32)
    sel_base = jnp.zeros((ne, 1), F32)
    cnt = jnp.zeros((ne, 128), F32)
    for j in range(nblk):
        sl = slice(j * tb, (j + 1) * tb)
        vj = vb[:, sl]
        eq = jnp.where(vj == thr, 1.0, 0.0)
        eq_rank = _dot(eq.astype(BF16), tri) + eq_base
        sel = jnp.where((vj > thr) | ((vj == thr) & (eq_rank < need)), 1.0, 0.0)
        pos = _dot(sel.astype(BF16), tri) + sel_base
        pos_ref[:, sl] = jnp.where(sel > 0.0, pos, -1.0)
        cnt = jnp.where(lane == j, sel_base, cnt)
        eq_base = eq_base + jnp.sum(eq, axis=1, keepdims=True)
        sel_base = sel_base + jnp.sum(sel, axis=1, keepdims=True)
    cnt_ref[...] = cnt.astype(jnp.int32)


def expert_select(aff_t, batch, seq, cap):
    ne = aff_t.shape[0]
    nblk = seq // TOKEN_BLOCK
    assert nblk <= 128
    return pl.pallas_call(
        functools.partial(_select_body, cap=cap, nblk=nblk),
        grid=(batch,),
        in_specs=[pl.BlockSpec((ne, seq), lambda b: (0, b))],
        out_specs=[pl.BlockSpec((None, ne, seq), lambda b: (b, 0, 0)),
                   pl.BlockSpec((None, ne, 128), lambda b: (b, 0, 0))],
        out_shape=[jax.ShapeDtypeStruct((batch, ne, seq), F32),
                   jax.ShapeDtypeStruct((batch, ne, 128), jnp.int32)],
        compiler_params=_cp(("parallel",)),
    )(aff_t)


def _slot_window(cnt_ref, flat, sp, ws):
    w = jnp.minimum((cnt_ref[flat] // SLOT_ALIGN) * SLOT_ALIGN, sp - ws)
    return pl.multiple_of(w, SLOT_ALIGN)


def _gather_body(cnt_ref, pos_ref, h_ref, o_ref, *, nblk, ws, sp, ne):
    b, e = pl.program_id(0), pl.program_id(1)
    o_ref[...] = jnp.zeros_like(o_ref)
    slot = lax.broadcasted_iota(jnp.int32, (ws, TOKEN_BLOCK), 0).astype(F32)

    def blk(j, carry):
        w = _slot_window(cnt_ref, (b * ne + e) * nblk + j, sp, ws)
        rel = pos_ref[pl.ds(j, 1), :] - w.astype(F32)
        onehot = jnp.where(slot == rel, 1.0, 0.0).astype(BF16)
        hb = h_ref[pl.ds(pl.multiple_of(j * TOKEN_BLOCK, TOKEN_BLOCK), TOKEN_BLOCK), :]
        o_ref[pl.ds(w, ws), :] += _dot(onehot, hb).astype(o_ref.dtype)
        return carry

    lax.fori_loop(0, nblk, blk, 0)


def expert_gather(h2, pos, cnt, batch, seq, sp, ws):
    d = h2.shape[1]
    ne = pos.shape[1]
    nblk = seq // TOKEN_BLOCK
    return pl.pallas_call(
        functools.partial(_gather_body, nblk=nblk, ws=ws, sp=sp, ne=ne),
        grid_spec=pltpu.PrefetchScalarGridSpec(
            num_scalar_prefetch=1,
            grid=(batch, ne),
            in_specs=[pl.BlockSpec((None, None, nblk, TOKEN_BLOCK), lambda b, e, c: (b, e, 0, 0)),
                      pl.BlockSpec((seq, d), lambda b, e, c: (b, 0))],
            out_specs=pl.BlockSpec((None, None, sp, d), lambda b, e, c: (e, b, 0, 0))),
        out_shape=jax.ShapeDtypeStruct((ne, batch, sp, d), BF16),
        compiler_params=_cp(("parallel", "arbitrary")),
    )(cnt, pos.reshape(batch, ne, nblk, TOKEN_BLOCK), h2)


def _ffn_body(x_ref, wg_ref, wu_ref, wd_ref, o_ref, *, nf):
    x = x_ref[...]
    fs = wg_ref.shape[1] // nf
    acc = None
    for f in range(nf):
        sl = slice(f * fs, (f + 1) * fs)
        gate = _dot(x, wg_ref[:, sl])
        up = _dot(x, wu_ref[:, sl])
        h = (gate * jax.nn.sigmoid(gate) * up).astype(BF16)
        y = _dot(h, wd_ref[sl, :])
        acc = y if acc is None else acc + y
    o_ref[...] = acc.astype(o_ref.dtype)


def expert_ffn(xg, w_gate, w_up, w_down, tm):
    ne, r, d = xg.shape
    ff = w_gate.shape[2]
    return pl.pallas_call(
        functools.partial(_ffn_body, nf=2),
        grid=(ne, r // tm),
        in_specs=[pl.BlockSpec((None, tm, d), lambda e, i: (e, i, 0)),
                  pl.BlockSpec((None, d, ff), lambda e, i: (e, 0, 0)),
                  pl.BlockSpec((None, d, ff), lambda e, i: (e, 0, 0)),
                  pl.BlockSpec((None, ff, d), lambda e, i: (e, 0, 0))],
        out_specs=pl.BlockSpec((None, tm, d), lambda e, i: (e, i, 0)),
        out_shape=jax.ShapeDtypeStruct((ne, r, d), BF16),
        compiler_params=_cp(("parallel", "arbitrary"), vmem=60 * 1024 * 1024),
    )(xg, w_gate, w_up, w_down)


def _scatter_body(cnt_ref, pos_ref, aff_ref, y_ref, x_ref, g2_ref, o_ref, *, nblk, ws, sp, ne):
    b, e = pl.program_id(0), pl.program_id(2)

    @pl.when(e == 0)
    def _():
        o_ref[...] = jnp.zeros_like(o_ref)

    lane = lax.broadcasted_iota(jnp.int32, (TOKEN_BLOCK, ne), 1)
    slot = lax.broadcasted_iota(jnp.int32, (TOKEN_BLOCK, ws), 1).astype(F32)

    def blk(j, carry):
        w = _slot_window(cnt_ref, (b * ne + e) * nblk + j, sp, ws)
        rows = pl.ds(pl.multiple_of(j * TOKEN_BLOCK, TOKEN_BLOCK), TOKEN_BLOCK)
        mine = lane == e
        rel = jnp.sum(jnp.where(mine, pos_ref[rows, :], 0.0), axis=1, keepdims=True) - w.astype(F32)
        gate = jnp.sum(jnp.where(mine, aff_ref[rows, :], 0.0), axis=1, keepdims=True)
        onehot = jnp.where(slot == rel, 1.0, 0.0).astype(BF16)
        o_ref[rows, :] += gate * _dot(onehot, y_ref[pl.ds(w, ws), :])
        return carry

    lax.fori_loop(0, nblk, blk, 0)

    @pl.when(e == ne - 1)
    def _():
        o_ref[...] = x_ref[...] + g2_ref[...] * o_ref[...]


def expert_scatter(yg, pos_t, aff, cnt, x, g2, batch, seq, sp, ws, rows_per_mod):
    m, d = x.shape
    ne = yg.shape[0]
    nblk = seq // TOKEN_BLOCK
    td = 512
    return pl.pallas_call(
        functools.partial(_scatter_body, nblk=nblk, ws=ws, sp=sp, ne=ne),
        grid_spec=pltpu.PrefetchScalarGridSpec(
            num_scalar_prefetch=1,
            grid=(batch, d // td, ne),
            in_specs=[pl.BlockSpec((seq, ne), lambda b, k, e, c: (b, 0)),
                      pl.BlockSpec((seq, ne), lambda b, k, e, c: (b, 0)),
                      pl.BlockSpec((None, None, sp, td), lambda b, k, e, c: (e, b, 0, k)),
                      pl.BlockSpec((seq, td), lambda b, k, e, c: (b, k)),
                      pl.BlockSpec((None, 1, td), lambda b, k, e, c: (b * seq // rows_per_mod, 0, k))],
            out_specs=pl.BlockSpec((seq, td), lambda b, k, e, c: (b, k))),
        out_shape=jax.ShapeDtypeStruct((m, d), F32),
        compiler_params=_cp(("parallel", "parallel", "arbitrary")),
    )(cnt, pos_t, aff, yg, x, g2)


def expert_choice_ffn_residual(xm, h2, aff_t, g2, lw, batch, seq, rows_per_mod):
    ne = aff_t.shape[0]
    cap = CAPACITY_FACTOR * seq // ne
    nblk = seq // TOKEN_BLOCK
    if cap <= TOKEN_BLOCK:
        sp = -(-cap // SLOT_ALIGN) * SLOT_ALIGN
        ws_g = ws_s = sp
    else:
        sp = cap + SLOT_ALIGN
        ws_g = TOKEN_BLOCK + SLOT_ALIGN
        ws_s = 2 * TOKEN_BLOCK
    pos, cnt = expert_select(aff_t, batch, seq, cap)
    cnt = cnt[:, :, :nblk].reshape(-1)
    xg = expert_gather(h2, pos, cnt, batch, seq, sp, ws_g)
    rows = batch * sp
    tm = sp if sp > 256 else min(rows, 512)
    yg = expert_ffn(xg.reshape(ne, rows, -1), lw['w_gate'], lw['w_up'], lw['w_down'], tm)
    yg = yg.reshape(ne, batch, sp, -1)
    pos_t = jnp.transpose(pos, (0, 2, 1)).reshape(batch * seq, ne)
    aff = jnp.transpose(aff_t)
    return expert_scatter(yg, pos_t, aff, cnt, xm, g2, batch, seq, sp, ws_s, rows_per_mod)


def trunk_layer(x, mods, lw, ops, attn_fn, h0, batch, seq, rows_per_mod, qkv_dtype):
    sh1, sc1, g1, sh2, sc2, g2 = mods
    qkv, u = in_proj(x, sh1, sc1, lw['norm1'], lw['w_in'], lw['q_norm'], lw['k_norm'], rows_per_mod, qkv_dtype,
                     tm=min(1024, x.shape[0]))
    attn = attn_fn(qkv)
    y, fin = s5_mix(u, batch, seq, ops, h0)
    xm, h2, aff_t = out_proj(attn, y, x, g1, sh2, sc2, lw['w_glu'], lw['b_glu'], lw['w_out'], lw['norm2'],
                             lw['w_router_t'], rows_per_mod, tm=512)
    x_new = expert_choice_ffn_residual(xm, h2, aff_t, g2, lw, batch, seq, rows_per_mod)
    return x_new, qkv, fin


def kernel(x_prompt, x_sample, cache_k, cache_v, state_ssm_re, state_ssm_im, c, c_ctx, w_ada, b_ada, norm1_g, norm2_g, w_in, q_norm_g, k_norm_g, rel_pos_bias, ssm_a_re, ssm_a_im, ssm_log_dt, ssm_b_re, ssm_b_im, ssm_c_re, ssm_c_im, ssm_d, w_glu, b_glu, w_out, w_router, w_expert_gate, w_expert_up, w_expert_down):
    bp, lp, d = x_prompt.shape
    bs, ls, _ = x_sample.shape
    depth = w_in.shape[0]
    past = cache_k.shape[2]
    n_cond = 1 + bs
    r_cond = -(-n_cond // 8) * 8
    cond = jnp.concatenate([c_ctx[None, :], c, jnp.zeros((r_cond - n_cond, d), F32)], axis=0)
    mods = adaln_mods(cond, w_ada, b_ada)

    xp = x_prompt.reshape(bp * lp, d)
    xs = x_sample.reshape(bs * ls, d)
    ks_out, vs_out, hre_out, him_out = [], [], [], []
    for l in range(depth):
        lw = dict(norm1=norm1_g[l][None], norm2=norm2_g[l][None], w_in=w_in[l].astype(BF16),
                  q_norm=q_norm_g[l][None], k_norm=k_norm_g[l][None],
                  w_glu=w_glu[l].astype(BF16), b_glu=b_glu[l][None], w_out=w_out[l].astype(BF16),
                  w_router_t=jnp.transpose(w_router[l]).astype(BF16),
                  w_gate=w_expert_gate[l].astype(BF16), w_up=w_expert_up[l].astype(BF16),
                  w_down=w_expert_down[l].astype(BF16))
        sp_ = dict(ssm_a_re=ssm_a_re[l], ssm_a_im=ssm_a_im[l], ssm_log_dt=ssm_log_dt[l], ssm_b_re=ssm_b_re[l],
                   ssm_b_im=ssm_b_im[l], ssm_c_re=ssm_c_re[l], ssm_c_im=ssm_c_im[l], ssm_d=ssm_d[l])
        ops = ssm_operators(sp_, SSM_CHUNK)
        six = mods[l].reshape(r_cond, 6, 1, d)
        mod_ctx = [six[0:1, i] for i in range(6)]
        mod_lat = [six[1:1 + bs, i] for i in range(6)]

        zero_state = jnp.zeros((4, bp, N_STATE), F32)
        xp, qkv_p, fin = trunk_layer(xp, mod_ctx, lw, ops, functools.partial(context_attention, batch=bp, seq=lp),
                                     zero_state, bp, lp, bp * lp, F32)
        ks_out.append(qkv_p[:, D_ATTN:2 * D_ATTN].reshape(bp, lp, N_HEADS, HEAD_DIM))
        vs_out.append(qkv_p[:, 2 * D_ATTN:].reshape(bp, lp, N_HEADS, HEAD_DIM))
        fin = fin.reshape(2, 2, bp, N_GROUPS, STATE_DIM)
        hre_out.append(jnp.transpose(fin[:, 0], (1, 0, 2, 3)))
        him_out.append(jnp.transpose(fin[:, 1], (1, 0, 2, 3)))

        h0 = jnp.stack([state_ssm_re[:, l, 0], state_ssm_im[:, l, 0], state_ssm_re[:, l, 1], state_ssm_im[:, l, 1]])
        h0 = h0.astype(F32).reshape(4, bs, N_STATE)
        kc = cache_k[:, l].reshape(bs, past, D_ATTN).astype(BF16)
        vc = cache_v[:, l].reshape(bs, past, D_ATTN).astype(BF16)
        bias = natten_bias_table(rel_pos_bias[l])
        attn_fn = functools.partial(neighbourhood_attention, k_ctx=kc, v_ctx=vc, bias=bias, batch=bs, seq=ls)
        xs, _, _ = trunk_layer(xs, mod_lat, lw, ops, attn_fn, h0, bs, ls, ls, BF16)

    return (xp.reshape(bp, lp, d), xs.reshape(bs, ls, d),
            jnp.stack(ks_out, axis=1), jnp.stack(vs_out, axis=1),
            jnp.stack(hre_out, axis=1), jnp.stack(him_out, axis=1))
```

```python
import functools
import math

import numpy as np
import jax
import jax.numpy as jnp
from jax import lax
from jax.experimental import pallas as pl
from jax.experimental.pallas import tpu as pltpu

F32 = jnp.float32
BF16 = jnp.bfloat16
EPS = 1e-6
NEG_INF = -1e30

N_HEADS = 12
HEAD_DIM = 128
D_ATTN = N_HEADS * HEAD_DIM
SSM_GROUP = 16
N_GROUPS = 32
D_SSM = SSM_GROUP * N_GROUPS
STATE_DIM = 64
N_STATE = N_GROUPS * STATE_DIM
GRID_W = 64
WIN_H = 8
WIN_W = 16
N_EXPERTS = 16
CAPACITY_FACTOR = 2
SSM_CHUNK = 32
TOKEN_BLOCK = 128
SLOT_ALIGN = 16
VMEM_LIMIT = 56 * 1024 * 1024

_NT = (((1,), (1,)), ((), ()))


def _cp(sem, vmem=VMEM_LIMIT):
    return pltpu.CompilerParams(dimension_semantics=sem, vmem_limit_bytes=vmem)


def _dot(a, b, **kw):
    return jnp.dot(a, b, preferred_element_type=F32, **kw)


def _adaln_body(c_ref, w_ref, b_ref, o_ref):
    c = c_ref[...]
    s = (c * jax.nn.sigmoid(c)).astype(BF16)
    o_ref[...] = _dot(s, w_ref[...].astype(BF16)) + b_ref[...]


def adaln_mods(cond, w_ada, b_ada):
    depth, d, n = w_ada.shape
    r = cond.shape[0]
    tn = 1024
    return pl.pallas_call(
        _adaln_body,
        grid=(depth, n // tn),
        in_specs=[pl.BlockSpec((r, d), lambda l, j: (0, 0)),
                  pl.BlockSpec((None, d, tn), lambda l, j: (l, 0, j)),
                  pl.BlockSpec((None, 1, tn), lambda l, j: (l, 0, j))],
        out_specs=pl.BlockSpec((None, r, tn), lambda l, j: (l, 0, j)),
        out_shape=jax.ShapeDtypeStruct((depth, r, n), F32),
        name="adaln",
        compiler_params=_cp(("parallel", "parallel")),
    )(cond, w_ada, b_ada.reshape(depth, 1, n))


def _win_body(x_ref, sh_ref, sc_ref, g_ref, w_ref, qn_ref, kn_ref, qkv_ref, u_ref, h_scr, *, nq, tn):
    j = pl.program_id(1)

    @pl.when(j == 0)
    def _():
        x = x_ref[...]
        y = x * lax.rsqrt(jnp.mean(x * x, axis=-1, keepdims=True) + EPS) * g_ref[...]
        h_scr[...] = (y * (1.0 + sc_ref[...]) + sh_ref[...]).astype(BF16)

    acc = _dot(h_scr[...], w_ref[...])

    @pl.when(j < 2 * nq)
    def _():
        gain = jnp.where(j < nq, qn_ref[...], kn_ref[...])
        for c in range(tn // HEAD_DIM):
            sl = slice(c * HEAD_DIM, (c + 1) * HEAD_DIM)
            blk = acc[:, sl]
            r = blk * lax.rsqrt(jnp.mean(blk * blk, axis=-1, keepdims=True) + EPS) * gain
            qkv_ref[:, sl] = r.astype(qkv_ref.dtype)

    @pl.when((j >= 2 * nq) & (j < 3 * nq))
    def _():
        qkv_ref[...] = acc.astype(qkv_ref.dtype)

    @pl.when(j == 3 * nq)
    def _():
        u_ref[...] = acc


def in_proj(x, sh, sc, g, w, qn, kn, rows_per_mod, qkv_dtype, tm):
    m, d = x.shape
    tn = D_SSM
    nq = D_ATTN // tn
    nj = 3 * nq + 1
    mod_map = lambda i, j: (i * tm // rows_per_mod, 0, 0)
    return pl.pallas_call(
        functools.partial(_win_body, nq=nq, tn=tn),
        grid=(m // tm, nj),
        in_specs=[pl.BlockSpec((tm, d), lambda i, j: (i, 0)),
                  pl.BlockSpec((None, 1, d), mod_map),
                  pl.BlockSpec((None, 1, d), mod_map),
                  pl.BlockSpec((1, d), lambda i, j: (0, 0)),
                  pl.BlockSpec((d, tn), lambda i, j: (0, j)),
                  pl.BlockSpec((1, HEAD_DIM), lambda i, j: (0, 0)),
                  pl.BlockSpec((1, HEAD_DIM), lambda i, j: (0, 0))],
        out_specs=[pl.BlockSpec((tm, tn), lambda i, j: (i, jnp.minimum(j, 3 * nq - 1))),
                   pl.BlockSpec((tm, tn), lambda i, j: (i, 0))],
        out_shape=[jax.ShapeDtypeStruct((m, 3 * D_ATTN), qkv_dtype),
                   jax.ShapeDtypeStruct((m, D_SSM), F32)],
        scratch_shapes=[pltpu.VMEM((tm, d), BF16)],
        name="in_proj",
        compiler_params=_cp(("parallel", "arbitrary")),
    )(x, sh, sc, g, w, qn, kn)


def _ctx_attn_body(q_ref, k_ref, v_ref, o_ref):
    scale = HEAD_DIM ** -0.5
    for h in range(N_HEADS):
        sl = slice(h * HEAD_DIM, (h + 1) * HEAD_DIM)
        q = q_ref[:, sl].astype(BF16)
        k = k_ref[:, sl].astype(BF16)
        v = v_ref[:, sl].astype(BF16)
        s = lax.dot_general(q, k, _NT, preferred_element_type=F32) * scale
        e = jnp.exp(s - jnp.max(s, axis=-1, keepdims=True))
        den = jnp.sum(e, axis=-1, keepdims=True)
        o_ref[:, sl] = (_dot(e.astype(BF16), v) / den).astype(o_ref.dtype)


def context_attention(qkv, batch, seq):
    m = qkv.shape[0]
    return pl.pallas_call(
        _ctx_attn_body,
        grid=(batch,),
        in_specs=[pl.BlockSpec((seq, D_ATTN), lambda b: (b, 0)),
                  pl.BlockSpec((seq, D_ATTN), lambda b: (b, 1)),
                  pl.BlockSpec((seq, D_ATTN), lambda b: (b, 2))],
        out_specs=pl.BlockSpec((seq, D_ATTN), lambda b: (b, 0)),
        out_shape=jax.ShapeDtypeStruct((m, D_ATTN), BF16),
        name="ctx_attn",
        compiler_params=_cp(("parallel",)),
    )(qkv, qkv, qkv)


ROW_BLOCK = 8
KEY_ROWS = 2 * WIN_H


def natten_bias_table(rpb):
    qc = np.arange(GRID_W)[:, None]
    kc = np.arange(GRID_W)[None, :]
    cs = np.clip(qc - WIN_W // 2, 0, GRID_W - WIN_W)
    col_ok = (kc >= cs) & (kc < cs + WIN_W)
    dc = np.clip(kc - qc, -(WIN_W - 1), WIN_W - 1) + (WIN_W - 1)
    half = WIN_H // 2
    t = np.arange(3)[:, None, None]
    qi = np.arange(ROW_BLOCK)[None, :, None]
    ki = np.arange(KEY_ROWS)[None, None, :]
    lo = np.clip(qi + half * t - half, 0, KEY_ROWS - WIN_H)
    row_ok = (ki >= lo) & (ki < lo + WIN_H)
    dr = np.clip(ki - qi - half * t, -(WIN_H - 1), WIN_H - 1) + (WIN_H - 1)
    ok = row_ok[:, :, :, None, None] & col_ok[None, None, None]
    tab = rpb[:, dr][:, :, :, :, dc]
    tab = jnp.where(jnp.asarray(ok)[None], tab.astype(F32), NEG_INF)
    tab = jnp.transpose(tab, (0, 1, 2, 4, 3, 5))
    return tab.reshape(rpb.shape[0], 3, ROW_BLOCK * GRID_W, KEY_ROWS * GRID_W)


def _natten_body(q_ref, k_ref, v_ref, kc_ref, vc_ref, bias_ref, o_ref, *, rows):
    scale = HEAD_DIM ** -0.5
    nq = ROW_BLOCK * GRID_W
    nloc = KEY_ROWS * GRID_W

    def row_block(rb, carry):
        first = jnp.clip(rb * ROW_BLOCK - WIN_H // 2, 0, rows - KEY_ROWS)
        kind = (rb * ROW_BLOCK - first) // (WIN_H // 2)
        qs = pl.ds(pl.multiple_of(rb * nq, nq), nq)
        ks = pl.ds(pl.multiple_of(first * GRID_W, GRID_W), nloc)
        q = q_ref[qs, :]
        s_loc = lax.dot_general(q, k_ref[ks, :], _NT, preferred_element_type=F32) * scale + bias_ref[kind]
        s_ctx = lax.dot_general(q, kc_ref[...], _NT, preferred_element_type=F32) * scale
        m = jnp.maximum(jnp.max(s_loc, axis=-1, keepdims=True), jnp.max(s_ctx, axis=-1, keepdims=True))
        e_loc = jnp.exp(s_loc - m)
        e_ctx = jnp.exp(s_ctx - m)
        den = jnp.sum(e_loc, axis=-1, keepdims=True) + jnp.sum(e_ctx, axis=-1, keepdims=True)
        o = _dot(e_loc.astype(BF16), v_ref[ks, :]) + _dot(e_ctx.astype(BF16), vc_ref[...])
        o_ref[qs, :] = (o / den).astype(o_ref.dtype)
        return carry

    lax.fori_loop(0, rows // ROW_BLOCK, row_block, 0)


def neighbourhood_attention(qkv, k_ctx, v_ctx, bias, batch, seq):
    m = qkv.shape[0]
    rows = seq // GRID_W
    assert rows >= KEY_ROWS and rows % ROW_BLOCK == 0 and seq % GRID_W == 0
    past = k_ctx.shape[1]
    tok = pl.BlockSpec((seq, HEAD_DIM), lambda b, h: (b, h))
    return pl.pallas_call(
        functools.partial(_natten_body, rows=rows),
        grid=(batch, N_HEADS),
        in_specs=[tok,
                  pl.BlockSpec((seq, HEAD_DIM), lambda b, h: (b, N_HEADS + h)),
                  pl.BlockSpec((seq, HEAD_DIM), lambda b, h: (b, 2 * N_HEADS + h)),
                  pl.BlockSpec((None, past, HEAD_DIM), lambda b, h: (b, 0, h)),
                  pl.BlockSpec((None, past, HEAD_DIM), lambda b, h: (b, 0, h)),
                  pl.BlockSpec((None, 3, ROW_BLOCK * GRID_W, KEY_ROWS * GRID_W), lambda b, h: (h, 0, 0, 0))],
        out_specs=tok,
        out_shape=jax.ShapeDtypeStruct((m, D_ATTN), BF16),
        name="natten",
        compiler_params=_cp(("parallel", "parallel")),
    )(qkv, qkv, qkv, k_ctx, v_ctx, bias)


def _bmm_body(a_ref, b_ref, o_ref, *, precision):
    a = a_ref[...]
    b = b_ref[...]
    if precision is None:
        a = a.astype(BF16)
        b = b.astype(BF16)
    o_ref[...] = _dot(a, b, precision=precision)


def bmm(a, b, tm, precision=None):
    g, m, k = a.shape
    n = b.shape[2]
    return pl.pallas_call(
        functools.partial(_bmm_body, precision=precision),
        grid=(g, m // tm),
        in_specs=[pl.BlockSpec((None, tm, k), lambda gi, i: (gi, i, 0)),
                  pl.BlockSpec((None, k, n), lambda gi, i: (gi, 0, 0))],
        out_specs=pl.BlockSpec((None, tm, n), lambda gi, i: (gi, i, 0)),
        out_shape=jax.ShapeDtypeStruct((g, m, n), F32),
        name="bmm",
        compiler_params=_cp(("parallel", "parallel")),
    )(a, b)


def ssm_operators(p, t):
    g, pp, c = N_GROUPS, STATE_DIM, SSM_GROUP
    a_re = p['ssm_a_re'].astype(F32)
    a_im = p['ssm_a_im'].astype(F32)
    dt = jnp.exp(p['ssm_log_dt'].astype(F32))[..., None]
    tau = jnp.arange(t + 1, dtype=F32)[:, None, None, None]
    mag = jnp.exp(tau * (dt * a_re))
    ph = tau * (dt * a_im)
    pr, pi = mag * jnp.cos(ph), mag * jnp.sin(ph)
    ar, ai = pr[1], pi[1]
    den = a_re * a_re + a_im * a_im
    qr = ((ar - 1.0) * a_re + ai * a_im) / den
    qi = (ai * a_re - (ar - 1.0) * a_im) / den
    b_re, b_im = p['ssm_b_re'].astype(F32), p['ssm_b_im'].astype(F32)
    bbr = qr[..., None] * b_re - qi[..., None] * b_im
    bbi = qr[..., None] * b_im + qi[..., None] * b_re
    ctr = jnp.swapaxes(p['ssm_c_re'].astype(F32), -1, -2)
    cti = jnp.swapaxes(p['ssm_c_im'].astype(F32), -1, -2)

    bcr = bbr[..., :, None] * ctr[..., None, :] - bbi[..., :, None] * cti[..., None, :]
    bci = bbr[..., :, None] * cti[..., None, :] + bbi[..., :, None] * ctr[..., None, :]
    rhs = jnp.concatenate([bcr, bci], axis=2).reshape(2 * g, 2 * pp, c * c)
    pw = jnp.concatenate([pr[:t], -pi[:t]], axis=-1)
    lhs = jnp.transpose(pw, (1, 2, 0, 3)).reshape(2 * g, t, 2 * pp)
    kf = bmm(lhs, rhs, tm=t, precision=lax.Precision.HIGHEST).reshape(2, g, t, c, c)
    kcat = jnp.concatenate([jnp.flip(kf[1], axis=1)[:, :t - 1], kf[0][:, :1] + kf[1][:, :1], kf[0][:, 1:]], axis=1)
    idx = np.arange(t)[None, :] - np.arange(t)[:, None] + (t - 1)
    m_intra = jnp.transpose(kcat[:, idx], (0, 1, 3, 2, 4)).reshape(g, c * t, c * t)

    def state_in(pwr_r, pwr_i, d):
        br = jnp.swapaxes(bbr[d], -1, -2)[:, None]
        bi = jnp.swapaxes(bbi[d], -1, -2)[:, None]
        er = jnp.transpose(pwr_r, (1, 0, 2))[:, :, None]
        ei = jnp.transpose(pwr_i, (1, 0, 2))[:, :, None]
        return er * br - ei * bi, er * bi + ei * br

    wfr, wfi = state_in(pr[:t, 0][::-1], pi[:t, 0][::-1], 0)
    wbr, wbi = state_in(pr[:t, 1], pi[:t, 1], 1)
    w_state = jnp.concatenate([wfr, wfi, wbr, wbi], axis=-1).reshape(g, c * t, 4 * pp)

    def state_out(pwr_r, pwr_i, d):
        cr = jnp.swapaxes(ctr[d], -1, -2)[:, None]
        ci = jnp.swapaxes(cti[d], -1, -2)[:, None]
        er = jnp.transpose(pwr_r, (1, 0, 2))[:, :, None]
        ei = jnp.transpose(pwr_i, (1, 0, 2))[:, :, None]
        vr = cr * er - ci * ei
        vi = -(cr * ei + ci * er)
        to_rows = lambda v: jnp.transpose(v, (0, 3, 1, 2)).reshape(g, pp, c * t)
        return to_rows(vr), to_rows(vi)

    vfr, vfi = state_out(pr[1:t + 1, 0], pi[1:t + 1, 0], 0)
    vbr, vbi = state_out(pr[1:t + 1, 1][::-1], pi[1:t + 1, 1][::-1], 1)
    v_state = jnp.concatenate([vfr, vfi, vbr, vbi], axis=1)
    d_tile = jnp.tile(p['ssm_d'].astype(F32).reshape(g, 1, c), (1, 1, t))
    a_chunk = jnp.stack([pr[t, 0], pi[t, 0], pr[t, 1], pi[t, 1]]).reshape(4, 1, g * pp)
    return w_state.astype(BF16), m_intra.astype(BF16), v_state.astype(BF16), d_tile, a_chunk


def _ssm_rec_body(loc_ref, a_ref, h0_ref, prev_ref, fin_ref, *, nc):
    def run(o, reverse):
        ar, ai = a_ref[o], a_ref[o + 1]

        def step(i, carry):
            n = nc - 1 - i if reverse else i
            sr, si = carry
            prev_ref[o, n] = sr
            prev_ref[o + 1, n] = si
            return ar * sr - ai * si + loc_ref[o, n], ar * si + ai * sr + loc_ref[o + 1, n]

        sr, si = lax.fori_loop(0, nc, step, (h0_ref[o], h0_ref[o + 1]))
        fin_ref[o] = sr
        fin_ref[o + 1] = si

    run(0, False)
    run(2, True)


def ssm_chunk_recurrence(loc, a_chunk, h0):
    _, nc, b, s = loc.shape
    tl = 256
    return pl.pallas_call(
        functools.partial(_ssm_rec_body, nc=nc),
        grid=(s // tl,),
        in_specs=[pl.BlockSpec((4, nc, b, tl), lambda i: (0, 0, 0, i)),
                  pl.BlockSpec((4, 1, tl), lambda i: (0, 0, i)),
                  pl.BlockSpec((4, b, tl), lambda i: (0, 0, i))],
        out_specs=[pl.BlockSpec((4, nc, b, tl), lambda i: (0, 0, 0, i)),
                   pl.BlockSpec((4, b, tl), lambda i: (0, 0, i))],
        out_shape=[jax.ShapeDtypeStruct(loc.shape, F32), jax.ShapeDtypeStruct(h0.shape, F32)],
        name="ssm_rec",
        compiler_params=_cp(("parallel",)),
    )(loc, a_chunk, h0)


def _ssm_out_body(u_ref, s_ref, m_ref, v_ref, d_ref, o_ref):
    u = u_ref[...]
    y = _dot(u.astype(BF16), m_ref[...]) + _dot(s_ref[...].astype(BF16), v_ref[...])
    o_ref[...] = y + d_ref[...] * u


def ssm_chunk_output(uc, sp, m_intra, v_state, d_tile, tm):
    g, n, k = uc.shape
    ks = sp.shape[2]
    return pl.pallas_call(
        _ssm_out_body,
        grid=(g, n // tm),
        in_specs=[pl.BlockSpec((None, tm, k), lambda gi, i: (gi, i, 0)),
                  pl.BlockSpec((None, tm, ks), lambda gi, i: (gi, i, 0)),
                  pl.BlockSpec((None, k, k), lambda gi, i: (gi, 0, 0)),
                  pl.BlockSpec((None, ks, k), lambda gi, i: (gi, 0, 0)),
                  pl.BlockSpec((None, 1, k), lambda gi, i: (gi, 0, 0))],
        out_specs=pl.BlockSpec((None, tm, k), lambda gi, i: (gi, i, 0)),
        out_shape=jax.ShapeDtypeStruct((g, n, k), F32),
        name="ssm_out",
        compiler_params=_cp(("parallel", "parallel")),
    )(uc, sp, m_intra, v_state, d_tile)


def s5_mix(u, batch, seq, ops, h0):
    w_state, m_intra, v_state, d_tile, a_chunk = ops
    t, g, c, pp = SSM_CHUNK, N_GROUPS, SSM_GROUP, STATE_DIM
    nc = seq // t
    n = batch * nc
    tm = min(n, 512)
    uc = jnp.transpose(u.reshape(batch, nc, t, g, c), (3, 0, 1, 2, 4)).reshape(g, n, t * c)
    loc = bmm(uc, w_state, tm=tm)
    loc = jnp.transpose(loc.reshape(g, batch, nc, 4, pp), (3, 2, 1, 0, 4)).reshape(4, nc, batch, g * pp)
    prev, fin = ssm_chunk_recurrence(loc, a_chunk, h0)
    sp = jnp.transpose(prev.reshape(4, nc, batch, g, pp), (3, 2, 1, 0, 4)).reshape(g, n, 4 * pp)
    y = ssm_chunk_output(uc, sp, m_intra, v_state, d_tile, tm)
    y = jnp.transpose(y.reshape(g, batch, nc, t, c), (1, 2, 3, 0, 4)).reshape(batch * seq, g * c)
    return y, fin


def _gelu_tanh(x):
    return 0.5 * x * (1.0 + jnp.tanh(math.sqrt(2.0 / math.pi) * (x + 0.044715 * (x * x * x))))


def _wout_body(attn_ref, y_ref, x_ref, g1_ref, sh2_ref, sc2_ref, wglu_ref, bglu_ref, wo_ref, n2_ref, wr_ref,
               xm_ref, h2_ref, aff_ref):
    g = _gelu_tanh(y_ref[...])
    ssm = g * jax.nn.sigmoid(_dot(g.astype(BF16), wglu_ref[...]) + bglu_ref[...])
    o = _dot(attn_ref[...], wo_ref[:D_ATTN, :]) + _dot(ssm.astype(BF16), wo_ref[D_ATTN:, :])
    xm = x_ref[...] + g1_ref[...] * o
    xm_ref[...] = xm
    hn = xm * lax.rsqrt(jnp.mean(xm * xm, axis=-1, keepdims=True) + EPS) * n2_ref[...]
    h2 = (hn * (1.0 + sc2_ref[...]) + sh2_ref[...]).astype(BF16)
    h2_ref[...] = h2
    logits = lax.dot_general(wr_ref[...], h2, _NT, preferred_element_type=F32)
    e = jnp.exp(logits - jnp.max(logits, axis=0, keepdims=True))
    aff_ref[...] = e / jnp.sum(e, axis=0, keepdims=True)


def out_proj(attn, y, x, g1, sh2, sc2, w_glu, b_glu, w_out, n2, w_router_t, rows_per_mod, tm):
    m, d = x.shape
    ne = w_router_t.shape[0]
    mod_map = lambda i: (i * tm // rows_per_mod, 0, 0)
    const = lambda i: (0, 0)
    return pl.pallas_call(
        _wout_body,
        grid=(m // tm,),
        in_specs=[pl.BlockSpec((tm, D_ATTN), lambda i: (i, 0)),
                  pl.BlockSpec((tm, D_SSM), lambda i: (i, 0)),
                  pl.BlockSpec((tm, d), lambda i: (i, 0)),
                  pl.BlockSpec((None, 1, d), mod_map),
                  pl.BlockSpec((None, 1, d), mod_map),
                  pl.BlockSpec((None, 1, d), mod_map),
                  pl.BlockSpec((D_SSM, D_SSM), const),
                  pl.BlockSpec((1, D_SSM), const),
                  pl.BlockSpec((D_ATTN + D_SSM, d), const),
                  pl.BlockSpec((1, d), const),
                  pl.BlockSpec((ne, d), const)],
        out_specs=[pl.BlockSpec((tm, d), lambda i: (i, 0)),
                   pl.BlockSpec((tm, d), lambda i: (i, 0)),
                   pl.BlockSpec((ne, tm), lambda i: (0, i))],
        out_shape=[jax.ShapeDtypeStruct((m, d), F32),
                   jax.ShapeDtypeStruct((m, d), BF16),
                   jax.ShapeDtypeStruct((ne, m), F32)],
        name="out_proj",
        compiler_params=_cp(("parallel",)),
    )(attn, y, x, g1, sh2, sc2, w_glu, b_glu, w_out, n2, w_router_t)


def _select_body(a_ref, pos_ref, cnt_ref, *, cap, nblk):
    v = a_ref[...]
    vb = pltpu.bitcast(v, jnp.int32)
    ne = v.shape[0]

    def bit_step(i, thr):
        cand = thr | (jnp.int32(1) << (30 - i))
        n_ge = jnp.sum(jnp.where(vb >= cand, 1.0, 0.0), axis=1, keepdims=True)
        return jnp.where(n_ge >= cap, cand, thr)

    thr = lax.fori_loop(0, 31, bit_step, jnp.zeros((ne, 1), jnp.int32))
    need = cap - jnp.sum(jnp.where(vb > thr, 1.0, 0.0), axis=1, keepdims=True)
    tb = TOKEN_BLOCK
    tri = jnp.where(lax.broadcasted_iota(jnp.int32, (tb, tb), 0) < lax.broadcasted_iota(jnp.int32, (tb, tb), 1),
                    1.0, 0.0).astype(BF16)
    lane = lax.broadcasted_iota(jnp.int32, (ne, 128), 1)
    eq_base = jnp.zeros((ne, 1), F32)
    sel_base = jnp.zeros((ne, 1), F32)
    cnt = jnp.zeros((ne, 128), F32)
    for j in range(nblk):
        sl = slice(j * tb, (j + 1) * tb)
        vj = vb[:, sl]
        eq = jnp.where(vj == thr, 1.0, 0.0)
        eq_rank = _dot(eq.astype(BF16), tri) + eq_base
        sel = jnp.where((vj > thr) | ((vj == thr) & (eq_rank < need)), 1.0, 0.0)
        pos = _dot(sel.astype(BF16), tri) + sel_base
        pos_ref[:, sl] = jnp.where(sel > 0.0, pos, -1.0)
        cnt = jnp.where(lane == j, sel_base, cnt)
        eq_base = eq_base + jnp.sum(eq, axis=1, keepdims=True)
        sel_base = sel_base + jnp.sum(sel, axis=1, keepdims=True)
    cnt_ref[...] = cnt.astype(jnp.int32)


def expert_select(aff_t, batch, seq, cap):
    ne = aff_t.shape[0]
    nblk = seq // TOKEN_BLOCK
    assert nblk <= 128
    return pl.pallas_call(
        functools.partial(_select_body, cap=cap, nblk=nblk),
        grid=(batch,),
        in_specs=[pl.BlockSpec((ne, seq), lambda b: (0, b))],
        out_specs=[pl.BlockSpec((None, ne, seq), lambda b: (b, 0, 0)),
                   pl.BlockSpec((None, ne, 128), lambda b: (b, 0, 0))],
        out_shape=[jax.ShapeDtypeStruct((batch, ne, seq), F32),
                   jax.ShapeDtypeStruct((batch, ne, 128), jnp.int32)],
        name="expert_select",
        compiler_params=_cp(("parallel",)),
    )(aff_t)


def _slot_window(cnt_ref, flat, sp, ws):
    w = jnp.minimum((cnt_ref[flat] // SLOT_ALIGN) * SLOT_ALIGN, sp - ws)
    return pl.multiple_of(w, SLOT_ALIGN)


def _gather_body(cnt_ref, pos_ref, h_ref, o_ref, *, nblk, ws, sp, ne):
    b, e = pl.program_id(0), pl.program_id(1)
    o_ref[...] = jnp.zeros_like(o_ref)
    slot = lax.broadcasted_iota(jnp.int32, (ws, TOKEN_BLOCK), 0).astype(F32)

    def blk(j, carry):
        w = _slot_window(cnt_ref, (b * ne + e) * nblk + j, sp, ws)
        rel = pos_ref[pl.ds(j, 1), :] - w.astype(F32)
        onehot = jnp.where(slot == rel, 1.0, 0.0).astype(BF16)
        hb = h_ref[pl.ds(pl.multiple_of(j * TOKEN_BLOCK, TOKEN_BLOCK), TOKEN_BLOCK), :]
        o_ref[pl.ds(w, ws), :] += _dot(onehot, hb).astype(o_ref.dtype)
        return carry

    lax.fori_loop(0, nblk, blk, 0)


def expert_gather(h2, pos, cnt, batch, seq, sp, ws):
    d = h2.shape[1]
    ne = pos.shape[1]
    nblk = seq // TOKEN_BLOCK
    return pl.pallas_call(
        functools.partial(_gather_body, nblk=nblk, ws=ws, sp=sp, ne=ne),
        grid_spec=pltpu.PrefetchScalarGridSpec(
            num_scalar_prefetch=1,
            grid=(batch, ne),
            in_specs=[pl.BlockSpec((None, None, nblk, TOKEN_BLOCK), lambda b, e, c: (b, e, 0, 0)),
                      pl.BlockSpec((seq, d), lambda b, e, c: (b, 0))],
            out_specs=pl.BlockSpec((None, None, sp, d), lambda b, e, c: (e, b, 0, 0))),
        out_shape=jax.ShapeDtypeStruct((ne, batch, sp, d), BF16),
        name="expert_gather",
        compiler_params=_cp(("parallel", "arbitrary")),
    )(cnt, pos.reshape(batch, ne, nblk, TOKEN_BLOCK), h2)


def _ffn_body(x_ref, wg_ref, wu_ref, wd_ref, o_ref, *, nf):
    x = x_ref[...]
    fs = wg_ref.shape[1] // nf
    acc = None
    for f in range(nf):
        sl = slice(f * fs, (f + 1) * fs)
        gate = _dot(x, wg_ref[:, sl])
        up = _dot(x, wu_ref[:, sl])
        h = (gate * jax.nn.sigmoid(gate) * up).astype(BF16)
        y = _dot(h, wd_ref[sl, :])
        acc = y if acc is None else acc + y
    o_ref[...] = acc.astype(o_ref.dtype)


def expert_ffn(xg, w_gate, w_up, w_down, tm):
    ne, r, d = xg.shape
    ff = w_gate.shape[2]
    return pl.pallas_call(
        functools.partial(_ffn_body, nf=2),
        grid=(ne, r // tm),
        in_specs=[pl.BlockSpec((None, tm, d), lambda e, i: (e, i, 0)),
                  pl.BlockSpec((None, d, ff), lambda e, i: (e, 0, 0)),
                  pl.BlockSpec((None, d, ff), lambda e, i: (e, 0, 0)),
                  pl.BlockSpec((None, ff, d), lambda e, i: (e, 0, 0))],
        out_specs=pl.BlockSpec((None, tm, d), lambda e, i: (e, i, 0)),
        out_shape=jax.ShapeDtypeStruct((ne, r, d), BF16),
        name="expert_ffn",
        compiler_params=_cp(("parallel", "arbitrary"), vmem=60 * 1024 * 1024),
    )(xg, w_gate, w_up, w_down)


def _scatter_body(cnt_ref, pos_ref, aff_ref, y_ref, x_ref, g2_ref, o_ref, *, nblk, ws, sp, ne):
    b, e = pl.program_id(0), pl.program_id(2)

    @pl.when(e == 0)
    def _():
        o_ref[...] = jnp.zeros_like(o_ref)

    lane = lax.broadcasted_iota(jnp.int32, (TOKEN_BLOCK, ne), 1)
    slot = lax.broadcasted_iota(jnp.int32, (TOKEN_BLOCK, ws), 1).astype(F32)

    def blk(j, carry):
        w = _slot_window(cnt_ref, (b * ne + e) * nblk + j, sp, ws)
        rows = pl.ds(pl.multiple_of(j * TOKEN_BLOCK, TOKEN_BLOCK), TOKEN_BLOCK)
        mine = lane == e
        rel = jnp.sum(jnp.where(mine, pos_ref[rows, :], 0.0), axis=1, keepdims=True) - w.astype(F32)
        gate = jnp.sum(jnp.where(mine, aff_ref[rows, :], 0.0), axis=1, keepdims=True)
        onehot = jnp.where(slot == rel, 1.0, 0.0).astype(BF16)
        o_ref[rows, :] += gate * _dot(onehot, y_ref[pl.ds(w, ws), :])
        return carry

    lax.fori_loop(0, nblk, blk, 0)

    @pl.when(e == ne - 1)
    def _():
        o_ref[...] = x_ref[...] + g2_ref[...] * o_ref[...]


def expert_scatter(yg, pos_t, aff, cnt, x, g2, batch, seq, sp, ws, rows_per_mod):
    m, d = x.shape
    ne = yg.shape[0]
    nblk = seq // TOKEN_BLOCK
    td = 512
    return pl.pallas_call(
        functools.partial(_scatter_body, nblk=nblk, ws=ws, sp=sp, ne=ne),
        grid_spec=pltpu.PrefetchScalarGridSpec(
            num_scalar_prefetch=1,
            grid=(batch, d // td, ne),
            in_specs=[pl.BlockSpec((seq, ne), lambda b, k, e, c: (b, 0)),
                      pl.BlockSpec((seq, ne), lambda b, k, e, c: (b, 0)),
                      pl.BlockSpec((None, None, sp, td), lambda b, k, e, c: (e, b, 0, k)),
                      pl.BlockSpec((seq, td), lambda b, k, e, c: (b, k)),
                      pl.BlockSpec((None, 1, td), lambda b, k, e, c: (b * seq // rows_per_mod, 0, k))],
            out_specs=pl.BlockSpec((seq, td), lambda b, k, e, c: (b, k))),
        out_shape=jax.ShapeDtypeStruct((m, d), F32),
        name="expert_scatter",
        compiler_params=_cp(("parallel", "parallel", "arbitrary")),
    )(cnt, pos_t, aff, yg, x, g2)


def expert_choice_ffn_residual(xm, h2, aff_t, g2, lw, batch, seq, rows_per_mod):
    ne = aff_t.shape[0]
    cap = CAPACITY_FACTOR * seq // ne
    nblk = seq // TOKEN_BLOCK
    if cap <= TOKEN_BLOCK:
        sp = -(-cap // SLOT_ALIGN) * SLOT_ALIGN
        ws_g = ws_s = sp
    else:
        sp = cap + SLOT_ALIGN
        ws_g = TOKEN_BLOCK + SLOT_ALIGN
        ws_s = 2 * TOKEN_BLOCK
    pos, cnt = expert_select(aff_t, batch, seq, cap)
    cnt = cnt[:, :, :nblk].reshape(-1)
    xg = expert_gather(h2, pos, cnt, batch, seq, sp, ws_g)
    rows = batch * sp
    tm = sp if sp > 256 else min(rows, 512)
    yg = expert_ffn(xg.reshape(ne, rows, -1), lw['w_gate'], lw['w_up'], lw['w_down'], tm)
    yg = yg.reshape(ne, batch, sp, -1)
    pos_t = jnp.transpose(pos, (0, 2, 1)).reshape(batch * seq, ne)
    aff = jnp.transpose(aff_t)
    return expert_scatter(yg, pos_t, aff, cnt, xm, g2, batch, seq, sp, ws_s, rows_per_mod)


def trunk_layer(x, mods, lw, ops, attn_fn, h0, batch, seq, rows_per_mod, qkv_dtype):
    sh1, sc1, g1, sh2, sc2, g2 = mods
    qkv, u = in_proj(x, sh1, sc1, lw['norm1'], lw['w_in'], lw['q_norm'], lw['k_norm'], rows_per_mod, qkv_dtype,
                     tm=min(1024, x.shape[0]))
    attn = attn_fn(qkv)
    y, fin = s5_mix(u, batch, seq, ops, h0)
    xm, h2, aff_t = out_proj(attn, y, x, g1, sh2, sc2, lw['w_glu'], lw['b_glu'], lw['w_out'], lw['norm2'],
                             lw['w_router_t'], rows_per_mod, tm=512)
    x_new = expert_choice_ffn_residual(xm, h2, aff_t, g2, lw, batch, seq, rows_per_mod)
    return x_new, qkv, fin


def kernel(x_prompt, x_sample, cache_k, cache_v, state_ssm_re, state_ssm_im, c, c_ctx, w_ada, b_ada, norm1_g, norm2_g, w_in, q_norm_g, k_norm_g, rel_pos_bias, ssm_a_re, ssm_a_im, ssm_log_dt, ssm_b_re, ssm_b_im, ssm_c_re, ssm_c_im, ssm_d, w_glu, b_glu, w_out, w_router, w_expert_gate, w_expert_up, w_expert_down):
    bp, lp, d = x_prompt.shape
    bs, ls, _ = x_sample.shape
    depth = w_in.shape[0]
    past = cache_k.shape[2]
    n_cond = 1 + bs
    r_cond = -(-n_cond // 8) * 8
    cond = jnp.concatenate([c_ctx[None, :], c, jnp.zeros((r_cond - n_cond, d), F32)], axis=0)
    mods = adaln_mods(cond, w_ada, b_ada)

    xp = x_prompt.reshape(bp * lp, d)
    xs = x_sample.reshape(bs * ls, d)
    ks_out, vs_out, hre_out, him_out = [], [], [], []
    for l in range(depth):
        lw = dict(norm1=norm1_g[l][None], norm2=norm2_g[l][None], w_in=w_in[l].astype(BF16),
                  q_norm=q_norm_g[l][None], k_norm=k_norm_g[l][None],
                  w_glu=w_glu[l].astype(BF16), b_glu=b_glu[l][None], w_out=w_out[l].astype(BF16),
                  w_router_t=jnp.transpose(w_router[l]).astype(BF16),
                  w_gate=w_expert_gate[l].astype(BF16), w_up=w_expert_up[l].astype(BF16),
                  w_down=w_expert_down[l].astype(BF16))
        sp_ = dict(ssm_a_re=ssm_a_re[l], ssm_a_im=ssm_a_im[l], ssm_log_dt=ssm_log_dt[l], ssm_b_re=ssm_b_re[l],
                   ssm_b_im=ssm_b_im[l], ssm_c_re=ssm_c_re[l], ssm_c_im=ssm_c_im[l], ssm_d=ssm_d[l])
        ops = ssm_operators(sp_, SSM_CHUNK)
        six = mods[l].reshape(r_cond, 6, 1, d)
        mod_ctx = [six[0:1, i] for i in range(6)]
        mod_lat = [six[1:1 + bs, i] for i in range(6)]

        zero_state = jnp.zeros((4, bp, N_STATE), F32)
        xp, qkv_p, fin = trunk_layer(xp, mod_ctx, lw, ops, functools.partial(context_attention, batch=bp, seq=lp),
                                     zero_state, bp, lp, bp * lp, F32)
        ks_out.append(qkv_p[:, D_ATTN:2 * D_ATTN].reshape(bp, lp, N_HEADS, HEAD_DIM))
        vs_out.append(qkv_p[:, 2 * D_ATTN:].reshape(bp, lp, N_HEADS, HEAD_DIM))
        fin = fin.reshape(2, 2, bp, N_GROUPS, STATE_DIM)
        hre_out.append(jnp.transpose(fin[:, 0], (1, 0, 2, 3)))
        him_out.append(jnp.transpose(fin[:, 1], (1, 0, 2, 3)))

        h0 = jnp.stack([state_ssm_re[:, l, 0], state_ssm_im[:, l, 0], state_ssm_re[:, l, 1], state_ssm_im[:, l, 1]])
        h0 = h0.astype(F32).reshape(4, bs, N_STATE)
        kc = cache_k[:, l].reshape(bs, past, D_ATTN).astype(BF16)
        vc = cache_v[:, l].reshape(bs, past, D_ATTN).astype(BF16)
        bias = natten_bias_table(rel_pos_bias[l])
        attn_fn = functools.partial(neighbourhood_attention, k_ctx=kc, v_ctx=vc, bias=bias, batch=bs, seq=ls)
        xs, _, _ = trunk_layer(xs, mod_lat, lw, ops, attn_fn, h0, bs, ls, ls, BF16)

    return (xp.reshape(bp, lp, d), xs.reshape(bs, ls, d),
            jnp.stack(ks_out, axis=1), jnp.stack(vs_out, axis=1),
            jnp.stack(hre_out, axis=1), jnp.stack(him_out, axis=1))
```

```python
import functools
import math

import numpy as np
import jax
import jax.numpy as jnp
from jax import lax
from jax.experimental import pallas as pl
from jax.experimental.pallas import tpu as pltpu

F32 = jnp.float32
BF16 = jnp.bfloat16
EPS = 1e-6
NEG_INF = -1e30

N_HEADS = 12
HEAD_DIM = 128
D_ATTN = N_HEADS * HEAD_DIM
SSM_GROUP = 16
N_GROUPS = 32
D_SSM = SSM_GROUP * N_GROUPS
STATE_DIM = 64
N_STATE = N_GROUPS * STATE_DIM
GRID_W = 64
WIN_H = 8
WIN_W = 16
N_EXPERTS = 16
CAPACITY_FACTOR = 2
SSM_CHUNK = 32
TOKEN_BLOCK = 128
SLOT_ALIGN = 16
VMEM_LIMIT = 56 * 1024 * 1024

_NT = (((1,), (1,)), ((), ()))


def _cp(sem, vmem=VMEM_LIMIT):
    return pltpu.CompilerParams(dimension_semantics=sem, vmem_limit_bytes=vmem)


def _dot(a, b, **kw):
    return jnp.dot(a, b, preferred_element_type=F32, **kw)


def _adaln_body(c_ref, w_ref, b_ref, o_ref):
    c = c_ref[...]
    s = (c * jax.nn.sigmoid(c)).astype(BF16)
    o_ref[...] = _dot(s, w_ref[...].astype(BF16)) + b_ref[...]


def adaln_mods(cond, w_ada, b_ada):
    depth, d, n = w_ada.shape
    r = cond.shape[0]
    tn = 1024
    return pl.pallas_call(
        _adaln_body,
        grid=(depth, n // tn),
        in_specs=[pl.BlockSpec((r, d), lambda l, j: (0, 0)),
                  pl.BlockSpec((None, d, tn), lambda l, j: (l, 0, j)),
                  pl.BlockSpec((None, 1, tn), lambda l, j: (l, 0, j))],
        out_specs=pl.BlockSpec((None, r, tn), lambda l, j: (l, 0, j)),
        out_shape=jax.ShapeDtypeStruct((depth, r, n), F32),
        name="adaln",
        compiler_params=_cp(("parallel", "parallel")),
    )(cond, w_ada, b_ada.reshape(depth, 1, n))


def _win_body(x_ref, sh_ref, sc_ref, g_ref, w_ref, qn_ref, kn_ref, qkv_ref, u_ref, h_scr, *, nq, tn):
    j = pl.program_id(1)

    @pl.when(j == 0)
    def _():
        x = x_ref[...]
        y = x * lax.rsqrt(jnp.mean(x * x, axis=-1, keepdims=True) + EPS) * g_ref[...]
        h_scr[...] = (y * (1.0 + sc_ref[...]) + sh_ref[...]).astype(BF16)

    acc = _dot(h_scr[...], w_ref[...])

    @pl.when(j < 2 * nq)
    def _():
        gain = jnp.where(j < nq, qn_ref[...], kn_ref[...])
        for c in range(tn // HEAD_DIM):
            sl = slice(c * HEAD_DIM, (c + 1) * HEAD_DIM)
            blk = acc[:, sl]
            r = blk * lax.rsqrt(jnp.mean(blk * blk, axis=-1, keepdims=True) + EPS) * gain
            qkv_ref[:, sl] = r.astype(qkv_ref.dtype)

    @pl.when((j >= 2 * nq) & (j < 3 * nq))
    def _():
        qkv_ref[...] = acc.astype(qkv_ref.dtype)

    @pl.when(j == 3 * nq)
    def _():
        u_ref[...] = acc


def in_proj(x, sh, sc, g, w, qn, kn, rows_per_mod, qkv_dtype, tm):
    m, d = x.shape
    tn = D_SSM
    nq = D_ATTN // tn
    nj = 3 * nq + 1
    mod_map = lambda i, j: (i * tm // rows_per_mod, 0, 0)
    return pl.pallas_call(
        functools.partial(_win_body, nq=nq, tn=tn),
        grid=(m // tm, nj),
        in_specs=[pl.BlockSpec((tm, d), lambda i, j: (i, 0)),
                  pl.BlockSpec((None, 1, d), mod_map),
                  pl.BlockSpec((None, 1, d), mod_map),
                  pl.BlockSpec((1, d), lambda i, j: (0, 0)),
                  pl.BlockSpec((d, tn), lambda i, j: (0, j)),
                  pl.BlockSpec((1, HEAD_DIM), lambda i, j: (0, 0)),
                  pl.BlockSpec((1, HEAD_DIM), lambda i, j: (0, 0))],
        out_specs=[pl.BlockSpec((tm, tn), lambda i, j: (i, jnp.minimum(j, 3 * nq - 1))),
                   pl.BlockSpec((tm, tn), lambda i, j: (i, 0))],
        out_shape=[jax.ShapeDtypeStruct((m, 3 * D_ATTN), qkv_dtype),
                   jax.ShapeDtypeStruct((m, D_SSM), F32)],
        scratch_shapes=[pltpu.VMEM((tm, d), BF16)],
        name="in_proj",
        compiler_params=_cp(("parallel", "arbitrary")),
    )(x, sh, sc, g, w, qn, kn)


def _ctx_attn_body(q_ref, k_ref, v_ref, o_ref):
    scale = HEAD_DIM ** -0.5
    for h in range(N_HEADS):
        sl = slice(h * HEAD_DIM, (h + 1) * HEAD_DIM)
        q = q_ref[:, sl].astype(BF16)
        k = k_ref[:, sl].astype(BF16)
        v = v_ref[:, sl].astype(BF16)
        s = lax.dot_general(q, k, _NT, preferred_element_type=F32) * scale
        e = jnp.exp(s - jnp.max(s, axis=-1, keepdims=True))
        den = jnp.sum(e, axis=-1, keepdims=True)
        o_ref[:, sl] = (_dot(e.astype(BF16), v) / den).astype(o_ref.dtype)


def context_attention(qkv, batch, seq):
    m = qkv.shape[0]
    return pl.pallas_call(
        _ctx_attn_body,
        grid=(batch,),
        in_specs=[pl.BlockSpec((seq, D_ATTN), lambda b: (b, 0)),
                  pl.BlockSpec((seq, D_ATTN), lambda b: (b, 1)),
                  pl.BlockSpec((seq, D_ATTN), lambda b: (b, 2))],
        out_specs=pl.BlockSpec((seq, D_ATTN), lambda b: (b, 0)),
        out_shape=jax.ShapeDtypeStruct((m, D_ATTN), BF16),
        name="ctx_attn",
        compiler_params=_cp(("parallel",)),
    )(qkv, qkv, qkv)


ROW_BLOCK = 8
KEY_ROWS = 2 * WIN_H


def natten_bias_table(rpb):
    qc = np.arange(GRID_W)[:, None]
    kc = np.arange(GRID_W)[None, :]
    cs = np.clip(qc - WIN_W // 2, 0, GRID_W - WIN_W)
    col_ok = (kc >= cs) & (kc < cs + WIN_W)
    dc = np.clip(kc - qc, -(WIN_W - 1), WIN_W - 1) + (WIN_W - 1)
    tab = jnp.where(jnp.asarray(col_ok)[None, None], rpb[:, :, dc].astype(F32), NEG_INF)
    return jnp.concatenate([tab, tab], axis=-1)


def _natten_body(q_ref, k_ref, v_ref, kc_ref, vc_ref, tab_ref, o_ref, bias_ref, *, rows):
    scale = HEAD_DIM ** -0.5
    nq = ROW_BLOCK * GRID_W
    nloc = KEY_ROWS * GRID_W
    half = WIN_H // 2

    @pl.when(pl.program_id(1) == 0)
    def _():
        bias_ref[...] = jnp.full(bias_ref.shape, NEG_INF, F32)
        for t in range(3):
            for qi in range(ROW_BLOCK):
                lo = min(max(qi + half * t - half, 0), KEY_ROWS - WIN_H)
                for ki in range(lo, lo + WIN_H):
                    dr = ki - qi - half * t + (WIN_H - 1)
                    side = slice((ki % 2) * GRID_W, (ki % 2 + 1) * GRID_W)
                    bias_ref[t, qi * GRID_W:(qi + 1) * GRID_W, ki * GRID_W:(ki + 1) * GRID_W] = tab_ref[dr, :, side]

    def row_block(rb, carry):
        first = jnp.clip(rb * ROW_BLOCK - WIN_H // 2, 0, rows - KEY_ROWS)
        kind = (rb * ROW_BLOCK - first) // (WIN_H // 2)
        qs = pl.ds(pl.multiple_of(rb * nq, nq), nq)
        ks = pl.ds(pl.multiple_of(first * GRID_W, GRID_W), nloc)
        q = q_ref[qs, :]
        s_loc = lax.dot_general(q, k_ref[ks, :], _NT, preferred_element_type=F32) * scale + bias_ref[kind]
        s_ctx = lax.dot_general(q, kc_ref[...], _NT, preferred_element_type=F32) * scale
        m = jnp.maximum(jnp.max(s_loc, axis=-1, keepdims=True), jnp.max(s_ctx, axis=-1, keepdims=True))
        e_loc = jnp.exp(s_loc - m)
        e_ctx = jnp.exp(s_ctx - m)
        den = jnp.sum(e_loc, axis=-1, keepdims=True) + jnp.sum(e_ctx, axis=-1, keepdims=True)
        o = _dot(e_loc.astype(BF16), v_ref[ks, :]) + _dot(e_ctx.astype(BF16), vc_ref[...])
        o_ref[qs, :] = (o / den).astype(o_ref.dtype)
        return carry

    lax.fori_loop(0, rows // ROW_BLOCK, row_block, 0)


def neighbourhood_attention(qkv, k_ctx, v_ctx, bias, batch, seq):
    m = qkv.shape[0]
    rows = seq // GRID_W
    assert rows >= KEY_ROWS and rows % ROW_BLOCK == 0 and seq % GRID_W == 0
    past = k_ctx.shape[1]
    tok = pl.BlockSpec((seq, HEAD_DIM), lambda h, b: (b, h))
    return pl.pallas_call(
        functools.partial(_natten_body, rows=rows),
        grid=(N_HEADS, batch),
        in_specs=[tok,
                  pl.BlockSpec((seq, HEAD_DIM), lambda h, b: (b, N_HEADS + h)),
                  pl.BlockSpec((seq, HEAD_DIM), lambda h, b: (b, 2 * N_HEADS + h)),
                  pl.BlockSpec((None, past, HEAD_DIM), lambda h, b: (b, 0, h)),
                  pl.BlockSpec((None, past, HEAD_DIM), lambda h, b: (b, 0, h)),
                  pl.BlockSpec((None, 2 * WIN_H - 1, GRID_W, 2 * GRID_W), lambda h, b: (h, 0, 0, 0))],
        out_specs=tok,
        out_shape=jax.ShapeDtypeStruct((m, D_ATTN), BF16),
        scratch_shapes=[pltpu.VMEM((3, ROW_BLOCK * GRID_W, KEY_ROWS * GRID_W), F32)],
        name="natten",
        compiler_params=_cp(("parallel", "arbitrary")),
    )(qkv, qkv, qkv, k_ctx, v_ctx, bias)


def _bmm_body(a_ref, b_ref, o_ref, *, precision):
    a = a_ref[...]
    b = b_ref[...]
    if precision is None:
        a = a.astype(BF16)
        b = b.astype(BF16)
    o_ref[...] = _dot(a, b, precision=precision)


def bmm(a, b, tm, precision=None):
    g, m, k = a.shape
    n = b.shape[2]
    return pl.pallas_call(
        functools.partial(_bmm_body, precision=precision),
        grid=(g, m // tm),
        in_specs=[pl.BlockSpec((None, tm, k), lambda gi, i: (gi, i, 0)),
                  pl.BlockSpec((None, k, n), lambda gi, i: (gi, 0, 0))],
        out_specs=pl.BlockSpec((None, tm, n), lambda gi, i: (gi, i, 0)),
        out_shape=jax.ShapeDtypeStruct((g, m, n), F32),
        name="bmm",
        compiler_params=_cp(("parallel", "parallel")),
    )(a, b)


def ssm_operators(p, t):
    g, pp, c = N_GROUPS, STATE_DIM, SSM_GROUP
    a_re = p['ssm_a_re'].astype(F32)
    a_im = p['ssm_a_im'].astype(F32)
    dt = jnp.exp(p['ssm_log_dt'].astype(F32))[..., None]
    tau = jnp.arange(t + 1, dtype=F32)[:, None, None, None]
    mag = jnp.exp(tau * (dt * a_re))
    ph = tau * (dt * a_im)
    pr, pi = mag * jnp.cos(ph), mag * jnp.sin(ph)
    ar, ai = pr[1], pi[1]
    den = a_re * a_re + a_im * a_im
    qr = ((ar - 1.0) * a_re + ai * a_im) / den
    qi = (ai * a_re - (ar - 1.0) * a_im) / den
    b_re, b_im = p['ssm_b_re'].astype(F32), p['ssm_b_im'].astype(F32)
    bbr = qr[..., None] * b_re - qi[..., None] * b_im
    bbi = qr[..., None] * b_im + qi[..., None] * b_re
    ctr = jnp.swapaxes(p['ssm_c_re'].astype(F32), -1, -2)
    cti = jnp.swapaxes(p['ssm_c_im'].astype(F32), -1, -2)

    bcr = bbr[..., :, None] * ctr[..., None, :] - bbi[..., :, None] * cti[..., None, :]
    bci = bbr[..., :, None] * cti[..., None, :] + bbi[..., :, None] * ctr[..., None, :]
    rhs = jnp.concatenate([bcr, bci], axis=2).reshape(2 * g, 2 * pp, c * c)
    pw = jnp.concatenate([pr[:t], -pi[:t]], axis=-1)
    lhs = jnp.transpose(pw, (1, 2, 0, 3)).reshape(2 * g, t, 2 * pp)
    kf = bmm(lhs, rhs, tm=t, precision=lax.Precision.HIGHEST).reshape(2, g, t, c, c)
    kcat = jnp.concatenate([jnp.flip(kf[1], axis=1)[:, :t - 1], kf[0][:, :1] + kf[1][:, :1], kf[0][:, 1:]], axis=1)
    idx = np.arange(t)[None, :] - np.arange(t)[:, None] + (t - 1)
    m_intra = jnp.transpose(kcat[:, idx], (0, 1, 3, 2, 4)).reshape(g, c * t, c * t)

    def state_in(pwr_r, pwr_i, d):
        br = jnp.swapaxes(bbr[d], -1, -2)[:, None]
        bi = jnp.swapaxes(bbi[d], -1, -2)[:, None]
        er = jnp.transpose(pwr_r, (1, 0, 2))[:, :, None]
        ei = jnp.transpose(pwr_i, (1, 0, 2))[:, :, None]
        return er * br - ei * bi, er * bi + ei * br

    wfr, wfi = state_in(pr[:t, 0][::-1], pi[:t, 0][::-1], 0)
    wbr, wbi = state_in(pr[:t, 1], pi[:t, 1], 1)
    w_state = jnp.concatenate([wfr, wfi, wbr, wbi], axis=-1).reshape(g, c * t, 4 * pp)

    def state_out(pwr_r, pwr_i, d):
        cr = jnp.swapaxes(ctr[d], -1, -2)[:, None]
        ci = jnp.swapaxes(cti[d], -1, -2)[:, None]
        er = jnp.transpose(pwr_r, (1, 0, 2))[:, :, None]
        ei = jnp.transpose(pwr_i, (1, 0, 2))[:, :, None]
        vr = cr * er - ci * ei
        vi = -(cr * ei + ci * er)
        to_rows = lambda v: jnp.transpose(v, (0, 3, 1, 2)).reshape(g, pp, c * t)
        return to_rows(vr), to_rows(vi)

    vfr, vfi = state_out(pr[1:t + 1, 0], pi[1:t + 1, 0], 0)
    vbr, vbi = state_out(pr[1:t + 1, 1][::-1], pi[1:t + 1, 1][::-1], 1)
    v_state = jnp.concatenate([vfr, vfi, vbr, vbi], axis=1)
    d_tile = jnp.tile(p['ssm_d'].astype(F32).reshape(g, 1, c), (1, 1, t))
    a_chunk = jnp.stack([pr[t, 0], pi[t, 0], pr[t, 1], pi[t, 1]]).reshape(4, 1, g * pp)
    return w_state.astype(BF16), m_intra.astype(BF16), v_state.astype(BF16), d_tile, a_chunk


def _ssm_rec_body(loc_ref, a_ref, h0_ref, prev_ref, fin_ref, *, nc):
    def run(o, reverse):
        ar, ai = a_ref[o], a_ref[o + 1]

        def step(i, carry):
            n = nc - 1 - i if reverse else i
            sr, si = carry
            prev_ref[o, n] = sr
            prev_ref[o + 1, n] = si
            return ar * sr - ai * si + loc_ref[o, n], ar * si + ai * sr + loc_ref[o + 1, n]

        sr, si = lax.fori_loop(0, nc, step, (h0_ref[o], h0_ref[o + 1]))
        fin_ref[o] = sr
        fin_ref[o + 1] = si

    run(0, False)
    run(2, True)


def ssm_chunk_recurrence(loc, a_chunk, h0):
    _, nc, b, s = loc.shape
    tl = 256
    return pl.pallas_call(
        functools.partial(_ssm_rec_body, nc=nc),
        grid=(s // tl,),
        in_specs=[pl.BlockSpec((4, nc, b, tl), lambda i: (0, 0, 0, i)),
                  pl.BlockSpec((4, 1, tl), lambda i: (0, 0, i)),
                  pl.BlockSpec((4, b, tl), lambda i: (0, 0, i))],
        out_specs=[pl.BlockSpec((4, nc, b, tl), lambda i: (0, 0, 0, i)),
                   pl.BlockSpec((4, b, tl), lambda i: (0, 0, i))],
        out_shape=[jax.ShapeDtypeStruct(loc.shape, F32), jax.ShapeDtypeStruct(h0.shape, F32)],
        name="ssm_rec",
        compiler_params=_cp(("parallel",)),
    )(loc, a_chunk, h0)


def _ssm_out_body(u_ref, s_ref, m_ref, v_ref, d_ref, o_ref):
    u = u_ref[...]
    y = _dot(u.astype(BF16), m_ref[...]) + _dot(s_ref[...].astype(BF16), v_ref[...])
    o_ref[...] = y + d_ref[...] * u


def ssm_chunk_output(uc, sp, m_intra, v_state, d_tile, tm):
    g, n, k = uc.shape
    ks = sp.shape[2]
    return pl.pallas_call(
        _ssm_out_body,
        grid=(g, n // tm),
        in_specs=[pl.BlockSpec((None, tm, k), lambda gi, i: (gi, i, 0)),
                  pl.BlockSpec((None, tm, ks), lambda gi, i: (gi, i, 0)),
                  pl.BlockSpec((None, k, k), lambda gi, i: (gi, 0, 0)),
                  pl.BlockSpec((None, ks, k), lambda gi, i: (gi, 0, 0)),
                  pl.BlockSpec((None, 1, k), lambda gi, i: (gi, 0, 0))],
        out_specs=pl.BlockSpec((None, tm, k), lambda gi, i: (gi, i, 0)),
        out_shape=jax.ShapeDtypeStruct((g, n, k), F32),
        name="ssm_out",
        compiler_params=_cp(("parallel", "parallel")),
    )(uc, sp, m_intra, v_state, d_tile)


def s5_mix(u, batch, seq, ops, h0):
    w_state, m_intra, v_state, d_tile, a_chunk = ops
    t, g, c, pp = SSM_CHUNK, N_GROUPS, SSM_GROUP, STATE_DIM
    nc = seq // t
    n = batch * nc
    tm = min(n, 512)
    uc = jnp.transpose(u.reshape(batch, nc, t, g, c), (3, 0, 1, 2, 4)).reshape(g, n, t * c)
    loc = bmm(uc, w_state, tm=tm)
    loc = jnp.transpose(loc.reshape(g, batch, nc, 4, pp), (3, 2, 1, 0, 4)).reshape(4, nc, batch, g * pp)
    prev, fin = ssm_chunk_recurrence(loc, a_chunk, h0)
    sp = jnp.transpose(prev.reshape(4, nc, batch, g, pp), (3, 2, 1, 0, 4)).reshape(g, n, 4 * pp)
    y = ssm_chunk_output(uc, sp, m_intra, v_state, d_tile, tm)
    y = jnp.transpose(y.reshape(g, batch, nc, t, c), (1, 2, 3, 0, 4)).reshape(batch * seq, g * c)
    return y, fin


def _gelu_tanh(x):
    return 0.5 * x * (1.0 + jnp.tanh(math.sqrt(2.0 / math.pi) * (x + 0.044715 * (x * x * x))))


def _wout_body(attn_ref, y_ref, x_ref, g1_ref, sh2_ref, sc2_ref, wglu_ref, bglu_ref, wo_ref, n2_ref, wr_ref,
               xm_ref, h2_ref, aff_ref):
    g = _gelu_tanh(y_ref[...])
    ssm = g * jax.nn.sigmoid(_dot(g.astype(BF16), wglu_ref[...]) + bglu_ref[...])
    o = _dot(attn_ref[...], wo_ref[:D_ATTN, :]) + _dot(ssm.astype(BF16), wo_ref[D_ATTN:, :])
    xm = x_ref[...] + g1_ref[...] * o
    xm_ref[...] = xm
    hn = xm * lax.rsqrt(jnp.mean(xm * xm, axis=-1, keepdims=True) + EPS) * n2_ref[...]
    h2 = (hn * (1.0 + sc2_ref[...]) + sh2_ref[...]).astype(BF16)
    h2_ref[...] = h2
    logits = lax.dot_general(wr_ref[...], h2, _NT, preferred_element_type=F32)
    e = jnp.exp(logits - jnp.max(logits, axis=0, keepdims=True))
    aff_ref[...] = e / jnp.sum(e, axis=0, keepdims=True)


def out_proj(attn, y, x, g1, sh2, sc2, w_glu, b_glu, w_out, n2, w_router_t, rows_per_mod, tm):
    m, d = x.shape
    ne = w_router_t.shape[0]
    mod_map = lambda i: (i * tm // rows_per_mod, 0, 0)
    const = lambda i: (0, 0)
    return pl.pallas_call(
        _wout_body,
        grid=(m // tm,),
        in_specs=[pl.BlockSpec((tm, D_ATTN), lambda i: (i, 0)),
                  pl.BlockSpec((tm, D_SSM), lambda i: (i, 0)),
                  pl.BlockSpec((tm, d), lambda i: (i, 0)),
                  pl.BlockSpec((None, 1, d), mod_map),
                  pl.BlockSpec((None, 1, d), mod_map),
                  pl.BlockSpec((None, 1, d), mod_map),
                  pl.BlockSpec((D_SSM, D_SSM), const),
                  pl.BlockSpec((1, D_SSM), const),
                  pl.BlockSpec((D_ATTN + D_SSM, d), const),
                  pl.BlockSpec((1, d), const),
                  pl.BlockSpec((ne, d), const)],
        out_specs=[pl.BlockSpec((tm, d), lambda i: (i, 0)),
                   pl.BlockSpec((tm, d), lambda i: (i, 0)),
                   pl.BlockSpec((ne, tm), lambda i: (0, i))],
        out_shape=[jax.ShapeDtypeStruct((m, d), F32),
                   jax.ShapeDtypeStruct((m, d), BF16),
                   jax.ShapeDtypeStruct((ne, m), F32)],
        name="out_proj",
        compiler_params=_cp(("parallel",)),
    )(attn, y, x, g1, sh2, sc2, w_glu, b_glu, w_out, n2, w_router_t)


def _select_body(a_ref, pos_ref, cnt_ref, *, cap, nblk):
    v = a_ref[...]
    vb = pltpu.bitcast(v, jnp.int32)
    ne = v.shape[0]

    def bit_step(i, thr):
        cand = thr | (jnp.int32(1) << (30 - i))
        n_ge = jnp.sum(jnp.where(vb >= cand, 1.0, 0.0), axis=1, keepdims=True)
        return jnp.where(n_ge >= cap, cand, thr)

    thr = lax.fori_loop(0, 31, bit_step, jnp.zeros((ne, 1), jnp.int32))
    need = cap - jnp.sum(jnp.where(vb > thr, 1.0, 0.0), axis=1, keepdims=True)
    tb = TOKEN_BLOCK
    tri = jnp.where(lax.broadcasted_iota(jnp.int32, (tb, tb), 0) < lax.broadcasted_iota(jnp.int32, (tb, tb), 1),
                    1.0, 0.0).astype(BF16)
    lane = lax.broadcasted_iota(jnp.int32, (ne, 128), 1)
    eq_base = jnp.zeros((ne, 1), F32)
    sel_base = jnp.zeros((ne, 1), F32)
    cnt = jnp.zeros((ne, 128), F32)
    for j in range(nblk):
        sl = slice(j * tb, (j + 1) * tb)
        vj = vb[:, sl]
        eq = jnp.where(vj == thr, 1.0, 0.0)
        eq_rank = _dot(eq.astype(BF16), tri) + eq_base
        sel = jnp.where((vj > thr) | ((vj == thr) & (eq_rank < need)), 1.0, 0.0)
        pos = _dot(sel.astype(BF16), tri) + sel_base
        pos_ref[:, sl] = jnp.where(sel > 0.0, pos, -1.0)
        cnt = jnp.where(lane == j, sel_base, cnt)
        eq_base = eq_base + jnp.sum(eq, axis=1, keepdims=True)
        sel_base = sel_base + jnp.sum(sel, axis=1, keepdims=True)
    cnt_ref[...] = jnp.where(lane == nblk, sel_base, cnt).astype(jnp.int32)


def expert_select(aff_t, batch, seq, cap):
    ne = aff_t.shape[0]
    nblk = seq // TOKEN_BLOCK
    assert nblk < 128
    return pl.pallas_call(
        functools.partial(_select_body, cap=cap, nblk=nblk),
        grid=(batch,),
        in_specs=[pl.BlockSpec((ne, seq), lambda b: (0, b))],
        out_specs=[pl.BlockSpec((None, ne, seq), lambda b: (b, 0, 0)),
                   pl.BlockSpec((None, ne, 128), lambda b: (b, 0, 0))],
        out_shape=[jax.ShapeDtypeStruct((batch, ne, seq), F32),
                   jax.ShapeDtypeStruct((batch, ne, 128), jnp.int32)],
        name="expert_select",
        compiler_params=_cp(("parallel",)),
    )(aff_t)


def _cnt_at(cnt_ref, b, e, j, ne, nblk):
    return cnt_ref[(b * ne + e) * (nblk + 1) + j]


def _num_passes(cnt_ref, b, j, ne, nblk, ws):
    def need(e, m):
        first = (_cnt_at(cnt_ref, b, e, j, ne, nblk) // SLOT_ALIGN) * SLOT_ALIGN
        return jnp.maximum(m, (_cnt_at(cnt_ref, b, e, j + 1, ne, nblk) - first + ws - 1) // ws)

    return lax.fori_loop(0, ne, need, jnp.int32(0))


def _window_start(cnt_ref, b, e, j, p, ne, nblk, ws, sp):
    first = (_cnt_at(cnt_ref, b, e, j, ne, nblk) // SLOT_ALIGN) * SLOT_ALIGN
    return pl.multiple_of(jnp.minimum(first + p * ws, sp - ws), SLOT_ALIGN)


def _relative_slot(pos, c0, p, ws, sp):
    first = jnp.floor(c0 * (1.0 / SLOT_ALIGN)) * SLOT_ALIGN + ws * p
    start = jnp.minimum(first, float(sp - ws))
    return jnp.where((pos >= first) & (pos < start + ws), pos - start, -1.0)


def _gather_body(cnt_ref, cntv_ref, pos_ref, h_ref, o_ref, *, nblk, ws, sp, ne, jpc):
    b, jc = pl.program_id(0), pl.program_id(2)

    @pl.when(jc == 0)
    def _():
        o_ref[...] = jnp.zeros_like(o_ref)

    k = ne * ws
    rows_e = lax.broadcasted_iota(jnp.int32, (k, ne), 0) // ws
    spread = jnp.where(rows_e == lax.broadcasted_iota(jnp.int32, (k, ne), 1), 1.0, 0.0).astype(BF16)
    row_slot = (lax.broadcasted_iota(jnp.int32, (k, 1), 0) % ws).astype(F32)
    lane = lax.broadcasted_iota(jnp.int32, (ne, 128), 1)
    for jj in range(jpc):
        j = jc * jpc + jj
        sl = slice(jj * TOKEN_BLOCK, (jj + 1) * TOKEN_BLOCK)
        pos = pos_ref[:, sl]
        c0 = jnp.sum(jnp.where(lane == j, cntv_ref[...], 0.0), axis=1, keepdims=True)
        hb = h_ref[sl, :]

        def one_pass(p, carry):
            rel = _relative_slot(pos, c0, p.astype(F32), ws, sp)
            onehot = jnp.where(_dot(spread, rel.astype(BF16)) == row_slot, 1.0, 0.0).astype(BF16)
            got = _dot(onehot, hb)
            for e in range(ne):
                w = _window_start(cnt_ref, b, e, j, p, ne, nblk, ws, sp)
                o_ref[e, pl.ds(w, ws), :] += got[e * ws:(e + 1) * ws].astype(o_ref.dtype)
            return carry

        lax.fori_loop(0, _num_passes(cnt_ref, b, j, ne, nblk, ws), one_pass, 0)


def expert_gather(h2, pos, cnt, cntv, batch, seq, sp, ws, td, jpc):
    d = h2.shape[1]
    ne = pos.shape[1]
    nblk = seq // TOKEN_BLOCK
    njc = nblk // jpc
    tok = jpc * TOKEN_BLOCK
    return pl.pallas_call(
        functools.partial(_gather_body, nblk=nblk, ws=ws, sp=sp, ne=ne, jpc=jpc),
        grid_spec=pltpu.PrefetchScalarGridSpec(
            num_scalar_prefetch=1,
            grid=(batch, d // td, njc),
            in_specs=[pl.BlockSpec((None, ne, 128), lambda b, k, jc, c: (b, 0, 0)),
                      pl.BlockSpec((None, ne, tok), lambda b, k, jc, c: (b, 0, jc)),
                      pl.BlockSpec((tok, td), lambda b, k, jc, c: (b * njc + jc, k))],
            out_specs=pl.BlockSpec((ne, None, sp, td), lambda b, k, jc, c: (0, b, 0, k))),
        out_shape=jax.ShapeDtypeStruct((ne, batch, sp, d), BF16),
        name="expert_gather",
        compiler_params=_cp(("parallel", "parallel", "arbitrary")),
    )(cnt, cntv, pos, h2)


def _ffn_body(x_ref, wg_ref, wu_ref, wd_ref, o_ref, *, nf):
    x = x_ref[...]
    fs = wg_ref.shape[1] // nf
    acc = None
    for f in range(nf):
        sl = slice(f * fs, (f + 1) * fs)
        gate = _dot(x, wg_ref[:, sl])
        up = _dot(x, wu_ref[:, sl])
        h = (gate * jax.nn.sigmoid(gate) * up).astype(BF16)
        y = _dot(h, wd_ref[sl, :])
        acc = y if acc is None else acc + y
    o_ref[...] = acc.astype(o_ref.dtype)


def expert_ffn(xg, w_gate, w_up, w_down, tm):
    ne, r, d = xg.shape
    ff = w_gate.shape[2]
    return pl.pallas_call(
        functools.partial(_ffn_body, nf=2),
        grid=(ne, r // tm),
        in_specs=[pl.BlockSpec((None, tm, d), lambda e, i: (e, i, 0)),
                  pl.BlockSpec((None, d, ff), lambda e, i: (e, 0, 0)),
                  pl.BlockSpec((None, d, ff), lambda e, i: (e, 0, 0)),
                  pl.BlockSpec((None, ff, d), lambda e, i: (e, 0, 0))],
        out_specs=pl.BlockSpec((None, tm, d), lambda e, i: (e, i, 0)),
        out_shape=jax.ShapeDtypeStruct((ne, r, d), BF16),
        name="expert_ffn",
        compiler_params=_cp(("parallel", "arbitrary"), vmem=60 * 1024 * 1024),
    )(xg, w_gate, w_up, w_down)


def _scatter_body(cnt_ref, cntt_ref, pos_ref, aff_ref, y_ref, x_ref, g2_ref, o_ref, ycat, *, nblk, ws, sp, ne, jpc):
    b, jc = pl.program_id(0), pl.program_id(2)
    k = ne * ws
    cols_e = lax.broadcasted_iota(jnp.int32, (ne, k), 1) // ws
    spread = jnp.where(cols_e == lax.broadcasted_iota(jnp.int32, (ne, k), 0), 1.0, 0.0).astype(BF16)
    col_slot = (lax.broadcasted_iota(jnp.int32, (1, k), 1) % ws).astype(F32)
    for jj in range(jpc):
        j = jc * jpc + jj
        rows = slice(jj * TOKEN_BLOCK, (jj + 1) * TOKEN_BLOCK)
        pos = pos_ref[rows, :]
        gate = aff_ref[rows, :]
        g_hi = gate.astype(BF16)
        g_lo = (gate - g_hi.astype(F32)).astype(BF16)
        gx_hi = _dot(g_hi, spread)
        gx_lo = _dot(g_lo, spread)
        c0 = cntt_ref[pl.ds(j, 1), :]

        def one_pass(p, acc):
            rel = _relative_slot(pos, c0, p.astype(F32), ws, sp)
            hit = _dot(rel.astype(BF16), spread) == col_slot
            weights = jnp.concatenate([jnp.where(hit, gx_hi, 0.0), jnp.where(hit, gx_lo, 0.0)], axis=0).astype(BF16)
            for e in range(ne):
                w = _window_start(cnt_ref, b, e, j, p, ne, nblk, ws, sp)
                ycat[e * ws:(e + 1) * ws, :] = y_ref[e, pl.ds(w, ws), :]
            r = _dot(weights, ycat[...])
            return acc + r[:TOKEN_BLOCK] + r[TOKEN_BLOCK:]

        acc = lax.fori_loop(0, _num_passes(cnt_ref, b, j, ne, nblk, ws), one_pass,
                            jnp.zeros((TOKEN_BLOCK, o_ref.shape[1]), F32))
        o_ref[rows, :] = x_ref[rows, :] + g2_ref[...] * acc


def expert_scatter(yg, pos_t, aff, cnt, cntt, x, g2, batch, seq, sp, ws, td, jpc, rows_per_mod):
    m, d = x.shape
    ne = yg.shape[0]
    nblk = seq // TOKEN_BLOCK
    njc = nblk // jpc
    tok = jpc * TOKEN_BLOCK
    tokens = lambda b, k, jc, c: (b * njc + jc, 0)
    return pl.pallas_call(
        functools.partial(_scatter_body, nblk=nblk, ws=ws, sp=sp, ne=ne, jpc=jpc),
        grid_spec=pltpu.PrefetchScalarGridSpec(
            num_scalar_prefetch=1,
            grid=(batch, d // td, njc),
            in_specs=[pl.BlockSpec((None, nblk + 1, ne), lambda b, k, jc, c: (b, 0, 0)),
                      pl.BlockSpec((tok, ne), tokens),
                      pl.BlockSpec((tok, ne), tokens),
                      pl.BlockSpec((ne, None, sp, td), lambda b, k, jc, c: (0, b, 0, k)),
                      pl.BlockSpec((tok, td), lambda b, k, jc, c: (b * njc + jc, k)),
                      pl.BlockSpec((None, 1, td), lambda b, k, jc, c: (b * seq // rows_per_mod, 0, k))],
            out_specs=pl.BlockSpec((tok, td), lambda b, k, jc, c: (b * njc + jc, k)),
            scratch_shapes=[pltpu.VMEM((ne * ws, td), BF16)]),
        out_shape=jax.ShapeDtypeStruct((m, d), F32),
        name="expert_scatter",
        compiler_params=_cp(("parallel", "parallel", "arbitrary")),
    )(cnt, cntt, pos_t, aff, yg, x, g2)


MAX_SLOT_WINDOW = 48
EXPERT_RESIDENT_BYTES = 18 * 1024 * 1024


def expert_choice_ffn_residual(xm, h2, aff_t, g2, lw, batch, seq, rows_per_mod):
    ne = aff_t.shape[0]
    d = xm.shape[1]
    cap = CAPACITY_FACTOR * seq // ne
    nblk = seq // TOKEN_BLOCK
    sp = -(-cap // SLOT_ALIGN) * SLOT_ALIGN
    if sp > MAX_SLOT_WINDOW:
        sp += SLOT_ALIGN
    ws = min(MAX_SLOT_WINDOW, sp)
    td = d
    while ne * sp * td * 2 > EXPERT_RESIDENT_BYTES:
        td //= 2
    jpc = min(4, nblk)
    pos, cnt = expert_select(aff_t, batch, seq, cap)
    cntv = cnt.astype(F32)
    cnt = cnt[:, :, :nblk + 1]
    cnt_flat = cnt.reshape(-1)
    cntt = jnp.transpose(cnt, (0, 2, 1)).astype(F32)
    xg = expert_gather(h2, pos, cnt_flat, cntv, batch, seq, sp, ws, td, jpc)
    rows = batch * sp
    tm = sp if sp > 256 else min(rows, 512)
    yg = expert_ffn(xg.reshape(ne, rows, -1), lw['w_gate'], lw['w_up'], lw['w_down'], tm)
    yg = yg.reshape(ne, batch, sp, -1)
    pos_t = jnp.transpose(pos, (0, 2, 1)).reshape(batch * seq, ne)
    aff = jnp.transpose(aff_t)
    return expert_scatter(yg, pos_t, aff, cnt_flat, cntt, xm, g2, batch, seq, sp, ws, td, jpc, rows_per_mod)


def trunk_layer(x, mods, lw, ops, attn_fn, h0, batch, seq, rows_per_mod, qkv_dtype):
    sh1, sc1, g1, sh2, sc2, g2 = mods
    qkv, u = in_proj(x, sh1, sc1, lw['norm1'], lw['w_in'], lw['q_norm'], lw['k_norm'], rows_per_mod, qkv_dtype,
                     tm=min(1024, x.shape[0]))
    attn = attn_fn(qkv)
    y, fin = s5_mix(u, batch, seq, ops, h0)
    xm, h2, aff_t = out_proj(attn, y, x, g1, sh2, sc2, lw['w_glu'], lw['b_glu'], lw['w_out'], lw['norm2'],
                             lw['w_router_t'], rows_per_mod, tm=512)
    x_new = expert_choice_ffn_residual(xm, h2, aff_t, g2, lw, batch, seq, rows_per_mod)
    return x_new, qkv, fin


def kernel(x_prompt, x_sample, cache_k, cache_v, state_ssm_re, state_ssm_im, c, c_ctx, w_ada, b_ada, norm1_g, norm2_g, w_in, q_norm_g, k_norm_g, rel_pos_bias, ssm_a_re, ssm_a_im, ssm_log_dt, ssm_b_re, ssm_b_im, ssm_c_re, ssm_c_im, ssm_d, w_glu, b_glu, w_out, w_router, w_expert_gate, w_expert_up, w_expert_down):
    bp, lp, d = x_prompt.shape
    bs, ls, _ = x_sample.shape
    depth = w_in.shape[0]
    past = cache_k.shape[2]
    n_cond = 1 + bs
    r_cond = -(-n_cond // 8) * 8
    cond = jnp.concatenate([c_ctx[None, :], c, jnp.zeros((r_cond - n_cond, d), F32)], axis=0)
    mods = adaln_mods(cond, w_ada, b_ada)

    xp = x_prompt.reshape(bp * lp, d)
    xs = x_sample.reshape(bs * ls, d)
    ks_out, vs_out, hre_out, him_out = [], [], [], []
    for l in range(depth):
        lw = dict(norm1=norm1_g[l][None], norm2=norm2_g[l][None], w_in=w_in[l].astype(BF16),
                  q_norm=q_norm_g[l][None], k_norm=k_norm_g[l][None],
                  w_glu=w_glu[l].astype(BF16), b_glu=b_glu[l][None], w_out=w_out[l].astype(BF16),
                  w_router_t=jnp.transpose(w_router[l]).astype(BF16),
                  w_gate=w_expert_gate[l].astype(BF16), w_up=w_expert_up[l].astype(BF16),
                  w_down=w_expert_down[l].astype(BF16))
        sp_ = dict(ssm_a_re=ssm_a_re[l], ssm_a_im=ssm_a_im[l], ssm_log_dt=ssm_log_dt[l], ssm_b_re=ssm_b_re[l],
                   ssm_b_im=ssm_b_im[l], ssm_c_re=ssm_c_re[l], ssm_c_im=ssm_c_im[l], ssm_d=ssm_d[l])
        ops = ssm_operators(sp_, SSM_CHUNK)
        six = mods[l].reshape(r_cond, 6, 1, d)
        mod_ctx = [six[0:1, i] for i in range(6)]
        mod_lat = [six[1:1 + bs, i] for i in range(6)]

        zero_state = jnp.zeros((4, bp, N_STATE), F32)
        xp, qkv_p, fin = trunk_layer(xp, mod_ctx, lw, ops, functools.partial(context_attention, batch=bp, seq=lp),
                                     zero_state, bp, lp, bp * lp, F32)
        ks_out.append(qkv_p[:, D_ATTN:2 * D_ATTN].reshape(bp, lp, N_HEADS, HEAD_DIM))
        vs_out.append(qkv_p[:, 2 * D_ATTN:].reshape(bp, lp, N_HEADS, HEAD_DIM))
        fin = fin.reshape(2, 2, bp, N_GROUPS, STATE_DIM)
        hre_out.append(jnp.transpose(fin[:, 0], (1, 0, 2, 3)))
        him_out.append(jnp.transpose(fin[:, 1], (1, 0, 2, 3)))

        h0 = jnp.stack([state_ssm_re[:, l, 0], state_ssm_im[:, l, 0], state_ssm_re[:, l, 1], state_ssm_im[:, l, 1]])
        h0 = h0.astype(F32).reshape(4, bs, N_STATE)
        kc = cache_k[:, l].reshape(bs, past, D_ATTN).astype(BF16)
        vc = cache_v[:, l].reshape(bs, past, D_ATTN).astype(BF16)
        bias = natten_bias_table(rel_pos_bias[l])
        attn_fn = functools.partial(neighbourhood_attention, k_ctx=kc, v_ctx=vc, bias=bias, batch=bs, seq=ls)
        xs, _, _ = trunk_layer(xs, mod_lat, lw, ops, attn_fn, h0, bs, ls, ls, BF16)

    return (xp.reshape(bp, lp, d), xs.reshape(bs, ls, d),
            jnp.stack(ks_out, axis=1), jnp.stack(vs_out, axis=1),
            jnp.stack(hre_out, axis=1), jnp.stack(him_out, axis=1))
```

```python
import functools
import math

import numpy as np
import jax
import jax.numpy as jnp
from jax import lax
from jax.experimental import pallas as pl
from jax.experimental.pallas import tpu as pltpu

F32 = jnp.float32
BF16 = jnp.bfloat16
EPS = 1e-6
NEG_INF = -1e30
LOG2E = 1.4426950408889634

N_HEADS = 12
HEAD_DIM = 128
D_ATTN = N_HEADS * HEAD_DIM
SSM_GROUP = 16
N_GROUPS = 32
D_SSM = SSM_GROUP * N_GROUPS
STATE_DIM = 64
N_STATE = N_GROUPS * STATE_DIM
GRID_W = 64
WIN_H = 8
WIN_W = 16
N_EXPERTS = 16
CAPACITY_FACTOR = 2
SSM_CHUNK = 32
TOKEN_BLOCK = 128
SLOT_ALIGN = 16
VMEM_LIMIT = 56 * 1024 * 1024

_NT = (((1,), (1,)), ((), ()))


def _cp(sem, vmem=VMEM_LIMIT):
    return pltpu.CompilerParams(dimension_semantics=sem, vmem_limit_bytes=vmem)


def _dot(a, b, **kw):
    return jnp.dot(a, b, preferred_element_type=F32, **kw)


def _adaln_body(c_ref, w_ref, b_ref, o_ref):
    c = c_ref[...]
    s = (c * jax.nn.sigmoid(c)).astype(BF16)
    o_ref[...] = _dot(s, w_ref[...].astype(BF16)) + b_ref[...]


def adaln_mods(cond, w_ada, b_ada):
    depth, d, n = w_ada.shape
    r = cond.shape[0]
    tn = 1024
    return pl.pallas_call(
        _adaln_body,
        grid=(depth, n // tn),
        in_specs=[pl.BlockSpec((r, d), lambda l, j: (0, 0)),
                  pl.BlockSpec((None, d, tn), lambda l, j: (l, 0, j)),
                  pl.BlockSpec((None, 1, tn), lambda l, j: (l, 0, j))],
        out_specs=pl.BlockSpec((None, r, tn), lambda l, j: (l, 0, j)),
        out_shape=jax.ShapeDtypeStruct((depth, r, n), F32),
        name="adaln",
        compiler_params=_cp(("parallel", "parallel")),
    )(cond, w_ada, b_ada.reshape(depth, 1, n))


def _win_body(x_ref, sh_ref, sc_ref, g_ref, w_ref, qn_ref, kn_ref, *rest, nq, tn):
    q_ref, k_ref, v_ref, u_ref, h_scr = rest[-5:]
    j = pl.program_id(1)

    @pl.when(j == 0)
    def _():
        x = x_ref[...]
        y = x * lax.rsqrt(jnp.mean(x * x, axis=-1, keepdims=True) + EPS) * g_ref[...]
        h_scr[...] = (y * (1.0 + sc_ref[...]) + sh_ref[...]).astype(BF16)

    acc = _dot(h_scr[...], w_ref[...])

    def head_norm(gain_ref, dst):
        for c in range(tn // HEAD_DIM):
            sl = slice(c * HEAD_DIM, (c + 1) * HEAD_DIM)
            blk = acc[:, sl]
            r = blk * lax.rsqrt(jnp.mean(blk * blk, axis=-1, keepdims=True) + EPS) * gain_ref[...]
            dst[..., sl] = r.reshape(dst.shape[:-1] + (HEAD_DIM,)).astype(dst.dtype)

    @pl.when(j < nq)
    def _():
        head_norm(qn_ref, q_ref)

    @pl.when((j >= nq) & (j < 2 * nq))
    def _():
        head_norm(kn_ref, k_ref)

    @pl.when((j >= 2 * nq) & (j < 3 * nq))
    def _():
        v_ref[...] = acc.reshape(v_ref.shape).astype(v_ref.dtype)

    @pl.when(j == 3 * nq)
    def _():
        u_ref[...] = acc


def in_proj(x, sh, sc, g, w, qn, kn, rows_per_mod, tm, cache=None):
    m, d = x.shape
    tn = D_SSM
    nq = D_ATTN // tn
    nj = 3 * nq + 1
    mod_map = lambda i, j: (i * tm // rows_per_mod, 0, 0)
    col = lambda j, first: jnp.clip(j - first, 0, nq - 1)
    in_specs = [pl.BlockSpec((tm, d), lambda i, j: (i, 0)),
                pl.BlockSpec((None, 1, d), mod_map),
                pl.BlockSpec((None, 1, d), mod_map),
                pl.BlockSpec((1, d), lambda i, j: (0, 0)),
                pl.BlockSpec((d, tn), lambda i, j: (0, j)),
                pl.BlockSpec((1, HEAD_DIM), lambda i, j: (0, 0)),
                pl.BlockSpec((1, HEAD_DIM), lambda i, j: (0, 0))]
    args = [x, sh, sc, g, w, qn, kn]
    aliases = {}
    if cache is None:
        kv_spec = lambda first: pl.BlockSpec((tm, tn), lambda i, j: (i, col(j, first)))
        kv_shape = jax.ShapeDtypeStruct((m, D_ATTN), BF16)
    else:
        layer, depth, seq, k_prev, v_prev = cache
        kv_spec = lambda first: pl.BlockSpec((tm // seq, None, seq, tn), lambda i, j: (i, layer, 0, col(j, first)))
        kv_shape = jax.ShapeDtypeStruct((m // seq, depth, seq, D_ATTN), F32)
        if k_prev is not None:
            in_specs += [pl.BlockSpec(memory_space=pl.ANY)] * 2
            aliases = {len(args): 1, len(args) + 1: 2}
            args += [k_prev, v_prev]
    return pl.pallas_call(
        functools.partial(_win_body, nq=nq, tn=tn),
        grid=(m // tm, nj),
        in_specs=in_specs,
        out_specs=[pl.BlockSpec((tm, tn), lambda i, j: (i, col(j, 0))),
                   kv_spec(nq),
                   kv_spec(2 * nq),
                   pl.BlockSpec((tm, tn), lambda i, j: (i, 0))],
        out_shape=[jax.ShapeDtypeStruct((m, D_ATTN), BF16), kv_shape, kv_shape,
                   jax.ShapeDtypeStruct((m, D_SSM), F32)],
        scratch_shapes=[pltpu.VMEM((tm, d), BF16)],
        input_output_aliases=aliases,
        name="in_proj",
        compiler_params=_cp(("parallel", "arbitrary")),
    )(*args)


def _ctx_attn_body(q_ref, k_ref, v_ref, o_ref):
    for h in range(N_HEADS):
        sl = slice(h * HEAD_DIM, (h + 1) * HEAD_DIM)
        q = q_ref[:, sl]
        k = k_ref[:, sl].astype(BF16)
        v = v_ref[:, sl].astype(BF16)
        s = lax.dot_general(q, k, _NT, preferred_element_type=F32)
        e = jnp.exp2(s - jnp.max(s, axis=-1, keepdims=True))
        den = jnp.sum(e, axis=-1, keepdims=True)
        o_ref[:, sl] = (_dot(e.astype(BF16), v) / den).astype(o_ref.dtype)


def context_attention(q, k_cache, v_cache, layer):
    m = q.shape[0]
    batch, _, seq, _ = k_cache.shape
    kv = pl.BlockSpec((None, None, seq, D_ATTN), lambda b: (b, layer, 0, 0))
    return pl.pallas_call(
        _ctx_attn_body,
        grid=(batch,),
        in_specs=[pl.BlockSpec((seq, D_ATTN), lambda b: (b, 0)), kv, kv],
        out_specs=pl.BlockSpec((seq, D_ATTN), lambda b: (b, 0)),
        out_shape=jax.ShapeDtypeStruct((m, D_ATTN), BF16),
        name="ctx_attn",
        compiler_params=_cp(("parallel",)),
    )(q, k_cache, v_cache)


ROW_BLOCK = 8
KEY_ROWS = 2 * WIN_H


def natten_bias_table(rpb):
    qc = np.arange(GRID_W)[:, None]
    kc = np.arange(GRID_W)[None, :]
    cs = np.clip(qc - WIN_W // 2, 0, GRID_W - WIN_W)
    col_ok = (kc >= cs) & (kc < cs + WIN_W)
    dc = np.clip(kc - qc, -(WIN_W - 1), WIN_W - 1) + (WIN_W - 1)
    tab = jnp.where(jnp.asarray(col_ok)[None, None], rpb[:, :, dc].astype(F32) * LOG2E, NEG_INF)
    return jnp.concatenate([tab, tab], axis=-1)


def _natten_body(q_ref, k_ref, v_ref, kc_ref, vc_ref, tab_ref, o_ref, bias_ref, s0_ref, e0_ref, inv0_ref,
                 s1_ref, e1_ref, inv1_ref, *, rows):
    nq = ROW_BLOCK * GRID_W
    nloc = KEY_ROWS * GRID_W
    half = WIN_H // 2

    @pl.when(pl.program_id(1) == 0)
    def _():
        bias_ref[...] = jnp.full(bias_ref.shape, NEG_INF, F32)
        for t in range(3):
            for qi in range(ROW_BLOCK):
                lo = min(max(qi + half * t - half, 0), KEY_ROWS - WIN_H)
                for ki in range(lo, lo + WIN_H):
                    dr = ki - qi - half * t + (WIN_H - 1)
                    side = slice((ki % 2) * GRID_W, (ki % 2 + 1) * GRID_W)
                    bias_ref[t, qi * GRID_W:(qi + 1) * GRID_W, ki * GRID_W:(ki + 1) * GRID_W] = tab_ref[dr, :, side]

    nb = rows // ROW_BLOCK
    bufs = ((s0_ref, e0_ref, inv0_ref), (s1_ref, e1_ref, inv1_ref))

    def key_rows(i):
        first = jnp.clip(i * ROW_BLOCK - half, 0, rows - KEY_ROWS)
        return pl.ds(pl.multiple_of(first * GRID_W, GRID_W), nloc)

    def queries(i):
        return pl.ds(pl.multiple_of(i * nq, nq), nq)

    def scores(i, par):
        s_ref = bufs[par][0]
        q = q_ref[queries(i), :]
        s_ref[:, :nloc] = lax.dot_general(q, k_ref[key_rows(i), :], _NT, preferred_element_type=F32)
        s_ref[:, nloc:] = lax.dot_general(q, kc_ref[...], _NT, preferred_element_type=F32)

    def softmax(par, t):
        s_ref, e_ref, inv_ref = bufs[par]
        for qi in range(ROW_BLOCK):
            lo = min(max(qi + half * t - half, 0), KEY_ROWS - WIN_H)
            c0, c1 = (lo // 2) * 2 * GRID_W, -(-(lo + WIN_H) // 2) * 2 * GRID_W
            r = slice(qi * GRID_W, (qi + 1) * GRID_W)
            s_loc = s_ref[r, c0:c1] + bias_ref[t, r, c0:c1]
            s_ctx = s_ref[r, nloc:]
            m = jnp.maximum(jnp.max(s_loc, axis=-1, keepdims=True), jnp.max(s_ctx, axis=-1, keepdims=True))
            e_loc = jnp.exp2(s_loc - m)
            e_ctx = jnp.exp2(s_ctx - m)
            inv_ref[r, :] = 1.0 / (jnp.sum(e_loc, axis=-1, keepdims=True) + jnp.sum(e_ctx, axis=-1, keepdims=True))
            if c0 > 0:
                e_ref[r, :c0] = jnp.zeros((GRID_W, c0), BF16)
            e_ref[r, c0:c1] = e_loc.astype(BF16)
            if c1 < nloc:
                e_ref[r, c1:nloc] = jnp.zeros((GRID_W, nloc - c1), BF16)
            e_ref[r, nloc:] = e_ctx.astype(BF16)

    def values(i, par):
        _, e_ref, inv_ref = bufs[par]
        o = _dot(e_ref[:, :nloc], v_ref[key_rows(i), :]) + _dot(e_ref[:, nloc:], vc_ref[...])
        o_ref[queries(i), :] = (o * inv_ref[...]).astype(o_ref.dtype)

    scores(0, 0)
    scores(1, 1)
    softmax(0, 0)

    def block_pair(j, carry):
        a = 2 * j + 1
        scores(a + 1, 0)
        softmax(1, 1)
        values(a - 1, 0)
        scores(a + 2, 1)
        softmax(0, 1)
        values(a, 1)
        return carry

    lax.fori_loop(0, (nb - 2) // 2, block_pair, 0)
    softmax(1, 2)
    values(nb - 2, 0)
    values(nb - 1, 1)


def neighbourhood_attention(q, k, v, k_ctx, v_ctx, bias, batch, seq):
    m = q.shape[0]
    rows = seq // GRID_W
    assert rows % (2 * ROW_BLOCK) == 0 and seq % GRID_W == 0
    past = k_ctx.shape[1]
    nq, nloc = ROW_BLOCK * GRID_W, KEY_ROWS * GRID_W
    tok = pl.BlockSpec((seq, HEAD_DIM), lambda h, b: (b, h))
    return pl.pallas_call(
        functools.partial(_natten_body, rows=rows),
        grid=(N_HEADS, batch),
        in_specs=[tok, tok, tok,
                  pl.BlockSpec((None, past, HEAD_DIM), lambda h, b: (b, 0, h)),
                  pl.BlockSpec((None, past, HEAD_DIM), lambda h, b: (b, 0, h)),
                  pl.BlockSpec((None, 2 * WIN_H - 1, GRID_W, 2 * GRID_W), lambda h, b: (h, 0, 0, 0))],
        out_specs=tok,
        out_shape=jax.ShapeDtypeStruct((m, D_ATTN), BF16),
        scratch_shapes=[pltpu.VMEM((3, nq, nloc), F32)] + 2 * [pltpu.VMEM((nq, nloc + past), F32),
                                                                pltpu.VMEM((nq, nloc + past), BF16),
                                                                pltpu.VMEM((nq, 1), F32)],
        name="natten",
        compiler_params=_cp(("parallel", "arbitrary")),
    )(q, k, v, k_ctx, v_ctx, bias)


def _bmm_body(a_ref, b_ref, o_ref, *, precision):
    a = a_ref[...]
    b = b_ref[...]
    if precision is None:
        a = a.astype(BF16)
        b = b.astype(BF16)
    o_ref[...] = _dot(a, b, precision=precision)


def bmm(a, b, tm, precision=None):
    g, m, k = a.shape
    n = b.shape[2]
    return pl.pallas_call(
        functools.partial(_bmm_body, precision=precision),
        grid=(g, m // tm),
        in_specs=[pl.BlockSpec((None, tm, k), lambda gi, i: (gi, i, 0)),
                  pl.BlockSpec((None, k, n), lambda gi, i: (gi, 0, 0))],
        out_specs=pl.BlockSpec((None, tm, n), lambda gi, i: (gi, i, 0)),
        out_shape=jax.ShapeDtypeStruct((g, m, n), F32),
        name="bmm",
        compiler_params=_cp(("parallel", "parallel")),
    )(a, b)


def ssm_operators(p, t):
    g, pp, c = N_GROUPS, STATE_DIM, SSM_GROUP
    a_re = p['ssm_a_re'].astype(F32)
    a_im = p['ssm_a_im'].astype(F32)
    dt = jnp.exp(p['ssm_log_dt'].astype(F32))[..., None]
    tau = jnp.arange(t + 1, dtype=F32)[:, None, None, None]
    mag = jnp.exp(tau * (dt * a_re))
    ph = tau * (dt * a_im)
    pr, pi = mag * jnp.cos(ph), mag * jnp.sin(ph)
    ar, ai = pr[1], pi[1]
    den = a_re * a_re + a_im * a_im
    qr = ((ar - 1.0) * a_re + ai * a_im) / den
    qi = (ai * a_re - (ar - 1.0) * a_im) / den
    b_re, b_im = p['ssm_b_re'].astype(F32), p['ssm_b_im'].astype(F32)
    bbr = qr[..., None] * b_re - qi[..., None] * b_im
    bbi = qr[..., None] * b_im + qi[..., None] * b_re
    ctr = jnp.swapaxes(p['ssm_c_re'].astype(F32), -1, -2)
    cti = jnp.swapaxes(p['ssm_c_im'].astype(F32), -1, -2)

    bcr = bbr[..., :, None] * ctr[..., None, :] - bbi[..., :, None] * cti[..., None, :]
    bci = bbr[..., :, None] * cti[..., None, :] + bbi[..., :, None] * ctr[..., None, :]
    rhs = jnp.concatenate([bcr, bci], axis=2).reshape(2 * g, 2 * pp, c * c)
    pw = jnp.concatenate([pr[:t], -pi[:t]], axis=-1)
    lhs = jnp.transpose(pw, (1, 2, 0, 3)).reshape(2 * g, t, 2 * pp)
    kf = bmm(lhs, rhs, tm=t, precision=lax.Precision.HIGHEST).reshape(2, g, t, c, c)
    kcat = jnp.concatenate([jnp.flip(kf[1], axis=1)[:, :t - 1], kf[0][:, :1] + kf[1][:, :1], kf[0][:, 1:]], axis=1)
    idx = np.arange(t)[None, :] - np.arange(t)[:, None] + (t - 1)
    m_intra = jnp.transpose(kcat[:, idx], (0, 1, 3, 2, 4)).reshape(g, c * t, c * t)

    def state_in(pwr_r, pwr_i, d):
        br = jnp.swapaxes(bbr[d], -1, -2)[:, None]
        bi = jnp.swapaxes(bbi[d], -1, -2)[:, None]
        er = jnp.transpose(pwr_r, (1, 0, 2))[:, :, None]
        ei = jnp.transpose(pwr_i, (1, 0, 2))[:, :, None]
        return er * br - ei * bi, er * bi + ei * br

    wfr, wfi = state_in(pr[:t, 0][::-1], pi[:t, 0][::-1], 0)
    wbr, wbi = state_in(pr[:t, 1], pi[:t, 1], 1)
    w_state = jnp.concatenate([wfr, wfi, wbr, wbi], axis=-1).reshape(g, c * t, 4 * pp)

    def state_out(pwr_r, pwr_i, d):
        cr = jnp.swapaxes(ctr[d], -1, -2)[:, None]
        ci = jnp.swapaxes(cti[d], -1, -2)[:, None]
        er = jnp.transpose(pwr_r, (1, 0, 2))[:, :, None]
        ei = jnp.transpose(pwr_i, (1, 0, 2))[:, :, None]
        vr = cr * er - ci * ei
        vi = -(cr * ei + ci * er)
        to_rows = lambda v: jnp.transpose(v, (0, 3, 1, 2)).reshape(g, pp, c * t)
        return to_rows(vr), to_rows(vi)

    vfr, vfi = state_out(pr[1:t + 1, 0], pi[1:t + 1, 0], 0)
    vbr, vbi = state_out(pr[1:t + 1, 1][::-1], pi[1:t + 1, 1][::-1], 1)
    v_state = jnp.concatenate([vfr, vfi, vbr, vbi], axis=1)
    d_tile = jnp.tile(p['ssm_d'].astype(F32).reshape(g, 1, c), (1, 1, t))
    a_chunk = jnp.stack([pr[t, 0], pi[t, 0], pr[t, 1], pi[t, 1]]).reshape(4, 1, g * pp)
    return w_state.astype(BF16), m_intra.astype(BF16), v_state.astype(BF16), d_tile, a_chunk


def _ssm_rec_body(loc_ref, a_ref, h0_ref, prev_ref, fin_ref, *, nc):
    def run(o, reverse):
        ar, ai = a_ref[o], a_ref[o + 1]

        def step(i, carry):
            n = nc - 1 - i if reverse else i
            sr, si = carry
            prev_ref[o, n] = sr
            prev_ref[o + 1, n] = si
            return ar * sr - ai * si + loc_ref[o, n], ar * si + ai * sr + loc_ref[o + 1, n]

        sr, si = lax.fori_loop(0, nc, step, (h0_ref[o], h0_ref[o + 1]))
        fin_ref[o] = sr
        fin_ref[o + 1] = si

    run(0, False)
    run(2, True)


def ssm_chunk_recurrence(loc, a_chunk, h0):
    _, nc, b, s = loc.shape
    tl = 256
    return pl.pallas_call(
        functools.partial(_ssm_rec_body, nc=nc),
        grid=(s // tl,),
        in_specs=[pl.BlockSpec((4, nc, b, tl), lambda i: (0, 0, 0, i)),
                  pl.BlockSpec((4, 1, tl), lambda i: (0, 0, i)),
                  pl.BlockSpec((4, b, tl), lambda i: (0, 0, i))],
        out_specs=[pl.BlockSpec((4, nc, b, tl), lambda i: (0, 0, 0, i)),
                   pl.BlockSpec((4, b, tl), lambda i: (0, 0, i))],
        out_shape=[jax.ShapeDtypeStruct(loc.shape, F32), jax.ShapeDtypeStruct(h0.shape, F32)],
        name="ssm_rec",
        compiler_params=_cp(("parallel",)),
    )(loc, a_chunk, h0)


def _ssm_out_body(u_ref, s_ref, m_ref, v_ref, d_ref, o_ref):
    u = u_ref[...]
    y = _dot(u.astype(BF16), m_ref[...]) + _dot(s_ref[...].astype(BF16), v_ref[...])
    o_ref[...] = y + d_ref[...] * u


def ssm_chunk_output(uc, sp, m_intra, v_state, d_tile, tm):
    g, n, k = uc.shape
    ks = sp.shape[2]
    return pl.pallas_call(
        _ssm_out_body,
        grid=(g, n // tm),
        in_specs=[pl.BlockSpec((None, tm, k), lambda gi, i: (gi, i, 0)),
                  pl.BlockSpec((None, tm, ks), lambda gi, i: (gi, i, 0)),
                  pl.BlockSpec((None, k, k), lambda gi, i: (gi, 0, 0)),
                  pl.BlockSpec((None, ks, k), lambda gi, i: (gi, 0, 0)),
                  pl.BlockSpec((None, 1, k), lambda gi, i: (gi, 0, 0))],
        out_specs=pl.BlockSpec((None, tm, k), lambda gi, i: (gi, i, 0)),
        out_shape=jax.ShapeDtypeStruct((g, n, k), F32),
        name="ssm_out",
        compiler_params=_cp(("parallel", "parallel")),
    )(uc, sp, m_intra, v_state, d_tile)


def s5_mix(u, batch, seq, ops, h0):
    w_state, m_intra, v_state, d_tile, a_chunk = ops
    t, g, c, pp = SSM_CHUNK, N_GROUPS, SSM_GROUP, STATE_DIM
    nc = seq // t
    n = batch * nc
    tm = min(n, 512)
    uc = jnp.transpose(u.reshape(batch, nc, t, g, c), (3, 0, 1, 2, 4)).reshape(g, n, t * c)
    loc = bmm(uc, w_state, tm=tm)
    loc = jnp.transpose(loc.reshape(g, batch, nc, 4, pp), (3, 2, 1, 0, 4)).reshape(4, nc, batch, g * pp)
    prev, fin = ssm_chunk_recurrence(loc, a_chunk, h0)
    sp = jnp.transpose(prev.reshape(4, nc, batch, g, pp), (3, 2, 1, 0, 4)).reshape(g, n, 4 * pp)
    y = ssm_chunk_output(uc, sp, m_intra, v_state, d_tile, tm)
    y = jnp.transpose(y.reshape(g, batch, nc, t, c), (1, 2, 3, 0, 4)).reshape(batch * seq, g * c)
    return y, fin


def _gelu_tanh(x):
    return 0.5 * x * (1.0 + jnp.tanh(math.sqrt(2.0 / math.pi) * (x + 0.044715 * (x * x * x))))


def _wout_body(attn_ref, y_ref, x_ref, g1_ref, sh2_ref, sc2_ref, wglu_ref, bglu_ref, wo_ref, n2_ref, wr_ref,
               xm_ref, h2_ref, aff_ref):
    g = _gelu_tanh(y_ref[...])
    ssm = g * jax.nn.sigmoid(_dot(g.astype(BF16), wglu_ref[...]) + bglu_ref[...])
    o = _dot(attn_ref[...], wo_ref[:D_ATTN, :]) + _dot(ssm.astype(BF16), wo_ref[D_ATTN:, :])
    xm = x_ref[...] + g1_ref[...] * o
    xm_ref[...] = xm
    hn = xm * lax.rsqrt(jnp.mean(xm * xm, axis=-1, keepdims=True) + EPS) * n2_ref[...]
    h2 = (hn * (1.0 + sc2_ref[...]) + sh2_ref[...]).astype(BF16)
    h2_ref[...] = h2
    logits = lax.dot_general(wr_ref[...], h2, _NT, preferred_element_type=F32)
    e = jnp.exp(logits - jnp.max(logits, axis=0, keepdims=True))
    aff_ref[...] = e / jnp.sum(e, axis=0, keepdims=True)


def out_proj(attn, y, x, g1, sh2, sc2, w_glu, b_glu, w_out, n2, w_router_t, rows_per_mod, tm):
    m, d = x.shape
    ne = w_router_t.shape[0]
    mod_map = lambda i: (i * tm // rows_per_mod, 0, 0)
    const = lambda i: (0, 0)
    return pl.pallas_call(
        _wout_body,
        grid=(m // tm,),
        in_specs=[pl.BlockSpec((tm, D_ATTN), lambda i: (i, 0)),
                  pl.BlockSpec((tm, D_SSM), lambda i: (i, 0)),
                  pl.BlockSpec((tm, d), lambda i: (i, 0)),
                  pl.BlockSpec((None, 1, d), mod_map),
                  pl.BlockSpec((None, 1, d), mod_map),
                  pl.BlockSpec((None, 1, d), mod_map),
                  pl.BlockSpec((D_SSM, D_SSM), const),
                  pl.BlockSpec((1, D_SSM), const),
                  pl.BlockSpec((D_ATTN + D_SSM, d), const),
                  pl.BlockSpec((1, d), const),
                  pl.BlockSpec((ne, d), const)],
        out_specs=[pl.BlockSpec((tm, d), lambda i: (i, 0)),
                   pl.BlockSpec((tm, d), lambda i: (i, 0)),
                   pl.BlockSpec((ne, tm), lambda i: (0, i))],
        out_shape=[jax.ShapeDtypeStruct((m, d), F32),
                   jax.ShapeDtypeStruct((m, d), BF16),
                   jax.ShapeDtypeStruct((ne, m), F32)],
        name="out_proj",
        compiler_params=_cp(("parallel",)),
    )(attn, y, x, g1, sh2, sc2, w_glu, b_glu, w_out, n2, w_router_t)


def _select_body(a_ref, pos_ref, cnt_ref, *, cap, nblk):
    v = a_ref[...]
    vb = pltpu.bitcast(v, jnp.int32)
    ne = v.shape[0]

    def bit_step(i, thr):
        cand = thr | (jnp.int32(1) << (30 - i))
        n_ge = jnp.sum(jnp.where(vb >= cand, 1.0, 0.0), axis=1, keepdims=True)
        return jnp.where(n_ge >= cap, cand, thr)

    thr = lax.fori_loop(0, 31, bit_step, jnp.zeros((ne, 1), jnp.int32))
    need = cap - jnp.sum(jnp.where(vb > thr, 1.0, 0.0), axis=1, keepdims=True)
    tb = TOKEN_BLOCK
    tri = jnp.where(lax.broadcasted_iota(jnp.int32, (tb, tb), 0) < lax.broadcasted_iota(jnp.int32, (tb, tb), 1),
                    1.0, 0.0).astype(BF16)
    lane = lax.broadcasted_iota(jnp.int32, (ne, 128), 1)
    eq_base = jnp.zeros((ne, 1), F32)
    sel_base = jnp.zeros((ne, 1), F32)
    cnt = jnp.zeros((ne, 128), F32)
    for j in range(nblk):
        sl = slice(j * tb, (j + 1) * tb)
        vj = vb[:, sl]
        eq = jnp.where(vj == thr, 1.0, 0.0)
        eq_rank = _dot(eq.astype(BF16), tri) + eq_base
        sel = jnp.where((vj > thr) | ((vj == thr) & (eq_rank < need)), 1.0, 0.0)
        pos = _dot(sel.astype(BF16), tri) + sel_base
        pos_ref[:, sl] = jnp.where(sel > 0.0, pos, -1.0)
        cnt = jnp.where(lane == j, sel_base, cnt)
        eq_base = eq_base + jnp.sum(eq, axis=1, keepdims=True)
        sel_base = sel_base + jnp.sum(sel, axis=1, keepdims=True)
    cnt_ref[...] = jnp.where(lane == nblk, sel_base, cnt).astype(jnp.int32)


def expert_select(aff_t, batch, seq, cap):
    ne = aff_t.shape[0]
    nblk = seq // TOKEN_BLOCK
    assert nblk < 128
    return pl.pallas_call(
        functools.partial(_select_body, cap=cap, nblk=nblk),
        grid=(batch,),
        in_specs=[pl.BlockSpec((ne, seq), lambda b: (0, b))],
        out_specs=[pl.BlockSpec((None, ne, seq), lambda b: (b, 0, 0)),
                   pl.BlockSpec((None, ne, 128), lambda b: (b, 0, 0))],
        out_shape=[jax.ShapeDtypeStruct((batch, ne, seq), F32),
                   jax.ShapeDtypeStruct((batch, ne, 128), jnp.int32)],
        name="expert_select",
        compiler_params=_cp(("parallel",)),
    )(aff_t)


def _cnt_at(cnt_ref, b, e, j, ne, nblk):
    return cnt_ref[(b * ne + e) * (nblk + 1) + j]


def window_passes(cnt, ws):
    first = (cnt[:, :, :-1] // SLOT_ALIGN) * SLOT_ALIGN
    return jnp.max((cnt[:, :, 1:] - first + ws - 1) // ws, axis=1).reshape(-1)


def _window_start(cnt_ref, b, e, j, p, ne, nblk, ws, sp):
    first = (_cnt_at(cnt_ref, b, e, j, ne, nblk) // SLOT_ALIGN) * SLOT_ALIGN
    return pl.multiple_of(jnp.minimum(first + p * ws, sp - ws), SLOT_ALIGN)


def _relative_slot(pos, c0, p, ws, sp):
    first = jnp.floor(c0 * (1.0 / SLOT_ALIGN)) * SLOT_ALIGN + ws * p
    start = jnp.minimum(first, float(sp - ws))
    return jnp.where((pos >= first) & (pos < start + ws), pos - start, -1.0)


def _gather_body(cnt_ref, npass_ref, cntv_ref, pos_ref, h_ref, o_ref, *, nblk, ws, sp, ne, jpc):
    b, jc = pl.program_id(0), pl.program_id(2)

    @pl.when(jc == 0)
    def _():
        o_ref[...] = jnp.zeros_like(o_ref)

    k = ne * ws
    rows_e = lax.broadcasted_iota(jnp.int32, (k, ne), 0) // ws
    spread = jnp.where(rows_e == lax.broadcasted_iota(jnp.int32, (k, ne), 1), 1.0, 0.0).astype(BF16)
    row_slot = (lax.broadcasted_iota(jnp.int32, (k, 1), 0) % ws).astype(F32)
    lane = lax.broadcasted_iota(jnp.int32, (ne, 128), 1)
    for jj in range(jpc):
        j = jc * jpc + jj
        sl = slice(jj * TOKEN_BLOCK, (jj + 1) * TOKEN_BLOCK)
        pos = pos_ref[:, sl]
        c0 = jnp.sum(jnp.where(lane == j, cntv_ref[...], 0.0), axis=1, keepdims=True)
        hb = h_ref[sl, :]

        def one_pass(p, carry):
            rel = _relative_slot(pos, c0, p.astype(F32), ws, sp)
            onehot = jnp.where(_dot(spread, rel.astype(BF16)) == row_slot, 1.0, 0.0).astype(BF16)
            got = _dot(onehot, hb)
            for e in range(ne):
                w = _window_start(cnt_ref, b, e, j, p, ne, nblk, ws, sp)
                o_ref[e, pl.ds(w, ws), :] += got[e * ws:(e + 1) * ws].astype(o_ref.dtype)
            return carry

        lax.fori_loop(0, npass_ref[b * nblk + j], one_pass, 0)


def expert_gather(h2, pos, cnt, npass, cntv, batch, seq, sp, ws, td, jpc):
    d = h2.shape[1]
    ne = pos.shape[1]
    nblk = seq // TOKEN_BLOCK
    njc = nblk // jpc
    tok = jpc * TOKEN_BLOCK
    return pl.pallas_call(
        functools.partial(_gather_body, nblk=nblk, ws=ws, sp=sp, ne=ne, jpc=jpc),
        grid_spec=pltpu.PrefetchScalarGridSpec(
            num_scalar_prefetch=2,
            grid=(batch, d // td, njc),
            in_specs=[pl.BlockSpec((None, ne, 128), lambda b, k, jc, c, n: (b, 0, 0)),
                      pl.BlockSpec((None, ne, tok), lambda b, k, jc, c, n: (b, 0, jc)),
                      pl.BlockSpec((tok, td), lambda b, k, jc, c, n: (b * njc + jc, k))],
            out_specs=pl.BlockSpec((ne, None, sp, td), lambda b, k, jc, c, n: (0, b, 0, k))),
        out_shape=jax.ShapeDtypeStruct((ne, batch, sp, d), BF16),
        name="expert_gather",
        compiler_params=_cp(("parallel", "parallel", "arbitrary")),
    )(cnt, npass, cntv, pos, h2)


def _ffn_body(x_ref, wg_ref, wu_ref, wd_ref, o_ref, *, nf):
    x = x_ref[...]
    fs = wg_ref.shape[1] // nf
    acc = None
    for f in range(nf):
        sl = slice(f * fs, (f + 1) * fs)
        gate = _dot(x, wg_ref[:, sl])
        up = _dot(x, wu_ref[:, sl])
        h = (gate * jax.nn.sigmoid(gate) * up).astype(BF16)
        y = _dot(h, wd_ref[sl, :])
        acc = y if acc is None else acc + y
    o_ref[...] = acc.astype(o_ref.dtype)


def expert_ffn(xg, w_gate, w_up, w_down, tm):
    ne, r, d = xg.shape
    ff = w_gate.shape[2]
    return pl.pallas_call(
        functools.partial(_ffn_body, nf=2),
        grid=(ne, r // tm),
        in_specs=[pl.BlockSpec((None, tm, d), lambda e, i: (e, i, 0)),
                  pl.BlockSpec((None, d, ff), lambda e, i: (e, 0, 0)),
                  pl.BlockSpec((None, d, ff), lambda e, i: (e, 0, 0)),
                  pl.BlockSpec((None, ff, d), lambda e, i: (e, 0, 0))],
        out_specs=pl.BlockSpec((None, tm, d), lambda e, i: (e, i, 0)),
        out_shape=jax.ShapeDtypeStruct((ne, r, d), BF16),
        name="expert_ffn",
        compiler_params=_cp(("parallel", "arbitrary"), vmem=60 * 1024 * 1024),
    )(xg, w_gate, w_up, w_down)


def _scatter_body(cnt_ref, npass_ref, cntt_ref, pos_ref, aff_ref, y_ref, x_ref, g2_ref, o_ref, ycat,
                  *, nblk, ws, sp, ne, jpc):
    b, jc = pl.program_id(0), pl.program_id(2)
    k = ne * ws
    cols_e = lax.broadcasted_iota(jnp.int32, (ne, k), 1) // ws
    spread = jnp.where(cols_e == lax.broadcasted_iota(jnp.int32, (ne, k), 0), 1.0, 0.0).astype(BF16)
    col_slot = (lax.broadcasted_iota(jnp.int32, (1, k), 1) % ws).astype(F32)
    for jj in range(jpc):
        j = jc * jpc + jj
        rows = slice(jj * TOKEN_BLOCK, (jj + 1) * TOKEN_BLOCK)
        pos = pos_ref[rows, :]
        gate = aff_ref[rows, :]
        g_hi = gate.astype(BF16)
        g_lo = (gate - g_hi.astype(F32)).astype(BF16)
        gx_hi = _dot(g_hi, spread)
        gx_lo = _dot(g_lo, spread)
        c0 = cntt_ref[pl.ds(j, 1), :]

        def one_pass(p, acc):
            rel = _relative_slot(pos, c0, p.astype(F32), ws, sp)
            hit = _dot(rel.astype(BF16), spread) == col_slot
            weights = jnp.concatenate([jnp.where(hit, gx_hi, 0.0), jnp.where(hit, gx_lo, 0.0)], axis=0).astype(BF16)
            for e in range(ne):
                w = _window_start(cnt_ref, b, e, j, p, ne, nblk, ws, sp)
                ycat[e * ws:(e + 1) * ws, :] = y_ref[e, pl.ds(w, ws), :]
            r = _dot(weights, ycat[...])
            return acc + r[:TOKEN_BLOCK] + r[TOKEN_BLOCK:]

        acc = lax.fori_loop(0, npass_ref[b * nblk + j], one_pass, jnp.zeros((TOKEN_BLOCK, o_ref.shape[1]), F32))
        o_ref[rows, :] = x_ref[rows, :] + g2_ref[...] * acc


def expert_scatter(yg, pos_t, aff, cnt, npass, cntt, x, g2, batch, seq, sp, ws, td, jpc, rows_per_mod):
    m, d = x.shape
    ne = yg.shape[0]
    nblk = seq // TOKEN_BLOCK
    njc = nblk // jpc
    tok = jpc * TOKEN_BLOCK
    tokens = lambda b, k, jc, c, n: (b * njc + jc, 0)
    return pl.pallas_call(
        functools.partial(_scatter_body, nblk=nblk, ws=ws, sp=sp, ne=ne, jpc=jpc),
        grid_spec=pltpu.PrefetchScalarGridSpec(
            num_scalar_prefetch=2,
            grid=(batch, d // td, njc),
            in_specs=[pl.BlockSpec((None, nblk + 1, ne), lambda b, k, jc, c, n: (b, 0, 0)),
                      pl.BlockSpec((tok, ne), tokens),
                      pl.BlockSpec((tok, ne), tokens),
                      pl.BlockSpec((ne, None, sp, td), lambda b, k, jc, c, n: (0, b, 0, k)),
                      pl.BlockSpec((tok, td), lambda b, k, jc, c, n: (b * njc + jc, k)),
                      pl.BlockSpec((None, 1, td), lambda b, k, jc, c, n: (b * seq // rows_per_mod, 0, k))],
            out_specs=pl.BlockSpec((tok, td), lambda b, k, jc, c, n: (b * njc + jc, k)),
            scratch_shapes=[pltpu.VMEM((ne * ws, td), BF16)]),
        out_shape=jax.ShapeDtypeStruct((m, d), F32),
        name="expert_scatter",
        compiler_params=_cp(("parallel", "parallel", "arbitrary")),
    )(cnt, npass, cntt, pos_t, aff, yg, x, g2)


MAX_SLOT_WINDOW = 48
EXPERT_RESIDENT_BYTES = 18 * 1024 * 1024


def expert_choice_ffn_residual(xm, h2, aff_t, g2, lw, batch, seq, rows_per_mod):
    ne = aff_t.shape[0]
    d = xm.shape[1]
    cap = CAPACITY_FACTOR * seq // ne
    nblk = seq // TOKEN_BLOCK
    sp = -(-cap // SLOT_ALIGN) * SLOT_ALIGN
    if sp > MAX_SLOT_WINDOW:
        sp += SLOT_ALIGN
    ws = min(MAX_SLOT_WINDOW, sp)
    td = d
    while ne * sp * td * 2 > EXPERT_RESIDENT_BYTES:
        td //= 2
    jpc = min(4, nblk)
    pos, cnt = expert_select(aff_t, batch, seq, cap)
    cntv = cnt.astype(F32)
    cnt = cnt[:, :, :nblk + 1]
    cnt_flat = cnt.reshape(-1)
    npass = window_passes(cnt, ws)
    cntt = jnp.transpose(cnt, (0, 2, 1)).astype(F32)
    xg = expert_gather(h2, pos, cnt_flat, npass, cntv, batch, seq, sp, ws, td, jpc)
    rows = batch * sp
    tm = sp if sp > 256 else min(rows, 512)
    yg = expert_ffn(xg.reshape(ne, rows, -1), lw['w_gate'], lw['w_up'], lw['w_down'], tm)
    yg = yg.reshape(ne, batch, sp, -1)
    pos_t = jnp.transpose(pos, (0, 2, 1)).reshape(batch * seq, ne)
    aff = jnp.transpose(aff_t)
    return expert_scatter(yg, pos_t, aff, cnt_flat, npass, cntt, xm, g2, batch, seq, sp, ws, td, jpc, rows_per_mod)


def trunk_layer(x, mods, lw, ops, attn_fn, h0, batch, seq, rows_per_mod, cache=None):
    sh1, sc1, g1, sh2, sc2, g2 = mods
    q, k, v, u = in_proj(x, sh1, sc1, lw['norm1'], lw['w_in'], lw['q_norm'], lw['k_norm'], rows_per_mod,
                         tm=min(1024, x.shape[0]), cache=cache)
    attn = attn_fn(q, k, v)
    y, fin = s5_mix(u, batch, seq, ops, h0)
    xm, h2, aff_t = out_proj(attn, y, x, g1, sh2, sc2, lw['w_glu'], lw['b_glu'], lw['w_out'], lw['norm2'],
                             lw['w_router_t'], rows_per_mod, tm=512)
    x_new = expert_choice_ffn_residual(xm, h2, aff_t, g2, lw, batch, seq, rows_per_mod)
    return x_new, k, v, fin


def kernel(x_prompt, x_sample, cache_k, cache_v, state_ssm_re, state_ssm_im, c, c_ctx, w_ada, b_ada, norm1_g, norm2_g, w_in, q_norm_g, k_norm_g, rel_pos_bias, ssm_a_re, ssm_a_im, ssm_log_dt, ssm_b_re, ssm_b_im, ssm_c_re, ssm_c_im, ssm_d, w_glu, b_glu, w_out, w_router, w_expert_gate, w_expert_up, w_expert_down):
    bp, lp, d = x_prompt.shape
    bs, ls, _ = x_sample.shape
    depth = w_in.shape[0]
    past = cache_k.shape[2]
    n_cond = 1 + bs
    r_cond = -(-n_cond // 8) * 8
    cond = jnp.concatenate([c_ctx[None, :], c, jnp.zeros((r_cond - n_cond, d), F32)], axis=0)
    mods = adaln_mods(cond, w_ada, b_ada)

    xp = x_prompt.reshape(bp * lp, d)
    xs = x_sample.reshape(bs * ls, d)
    new_k = new_v = None
    hre_out, him_out = [], []
    for l in range(depth):
        lw = dict(norm1=norm1_g[l][None], norm2=norm2_g[l][None], w_in=w_in[l].astype(BF16),
                  q_norm=q_norm_g[l][None] * (LOG2E * HEAD_DIM ** -0.5), k_norm=k_norm_g[l][None],
                  w_glu=w_glu[l].astype(BF16), b_glu=b_glu[l][None], w_out=w_out[l].astype(BF16),
                  w_router_t=jnp.transpose(w_router[l]).astype(BF16),
                  w_gate=w_expert_gate[l].astype(BF16), w_up=w_expert_up[l].astype(BF16),
                  w_down=w_expert_down[l].astype(BF16))
        sp_ = dict(ssm_a_re=ssm_a_re[l], ssm_a_im=ssm_a_im[l], ssm_log_dt=ssm_log_dt[l], ssm_b_re=ssm_b_re[l],
                   ssm_b_im=ssm_b_im[l], ssm_c_re=ssm_c_re[l], ssm_c_im=ssm_c_im[l], ssm_d=ssm_d[l])
        ops = ssm_operators(sp_, SSM_CHUNK)
        six = mods[l].reshape(r_cond, 6, 1, d)
        mod_ctx = [six[0:1, i] for i in range(6)]
        mod_lat = [six[1:1 + bs, i] for i in range(6)]

        zero_state = jnp.zeros((4, bp, N_STATE), F32)
        ctx_attn = functools.partial(context_attention, layer=l)
        xp, new_k, new_v, fin = trunk_layer(xp, mod_ctx, lw, ops, ctx_attn, zero_state, bp, lp, bp * lp,
                                            cache=(l, depth, lp, new_k, new_v))
        fin = fin.reshape(2, 2, bp, N_GROUPS, STATE_DIM)
        hre_out.append(jnp.transpose(fin[:, 0], (1, 0, 2, 3)))
        him_out.append(jnp.transpose(fin[:, 1], (1, 0, 2, 3)))

        h0 = jnp.stack([state_ssm_re[:, l, 0], state_ssm_im[:, l, 0], state_ssm_re[:, l, 1], state_ssm_im[:, l, 1]])
        h0 = h0.astype(F32).reshape(4, bs, N_STATE)
        kc = cache_k[:, l].reshape(bs, past, D_ATTN).astype(BF16)
        vc = cache_v[:, l].reshape(bs, past, D_ATTN).astype(BF16)
        bias = natten_bias_table(rel_pos_bias[l])
        lat_attn = functools.partial(neighbourhood_attention, k_ctx=kc, v_ctx=vc, bias=bias, batch=bs, seq=ls)
        xs, _, _, _ = trunk_layer(xs, mod_lat, lw, ops, lat_attn, h0, bs, ls, ls)

    return (xp.reshape(bp, lp, d), xs.reshape(bs, ls, d),
            new_k.reshape(bp, depth, lp, N_HEADS, HEAD_DIM), new_v.reshape(bp, depth, lp, N_HEADS, HEAD_DIM),
            jnp.stack(hre_out, axis=1), jnp.stack(him_out, axis=1))
```
